```python
import math
import jax
import jax.numpy as jnp
from jax import lax
import numpy as np

D_MODEL = 1024
BATCH = 2
SEQ = 16384
DEPTH = 4

N_HEADS = 4
HEAD_DIM = D_MODEL // N_HEADS
BLOCK = 128
D_FF = 2816
N_EXPERTS = 8
TOP_K = 2
D_FF_EXPERT = D_FF // 4
PLE_DIM = 256
N_A = DEPTH // 2
N_B = DEPTH - N_A
N_DENSE = (DEPTH + 1) // 2
N_MOE = DEPTH // 2
DEEPNORM_ALPHA = (2.0 * DEPTH) ** 0.25
DEEPNORM_BETA = (8.0 * DEPTH) ** -0.25
FORGET_BIAS_INIT = 2.0
LN_EPS = 1e-5

kernel_name = "yoco_stickbreak_fox_moe_deepnorm"


def layer_norm(x, g, b):
    xf = x.astype(jnp.float32)
    mu = jnp.mean(xf, axis=-1, keepdims=True)
    var = jnp.mean(jnp.square(xf - mu), axis=-1, keepdims=True)
    y = (xf - mu) * lax.rsqrt(var + LN_EPS) * g.astype(jnp.float32) + b.astype(jnp.float32)
    return y.astype(x.dtype)


def split_heads(t):
    b, s, _ = t.shape
    return t.reshape(b, s, N_HEADS, HEAD_DIM).transpose(0, 2, 1, 3)


def merge_heads(o):
    b, h, s, dh = o.shape
    return o.transpose(0, 2, 1, 3).reshape(b, s, h * dh)


def stick_breaking_attention(q, k, v):
    b, h, s_len, _ = q.shape
    nb = s_len // BLOCK
    scale = 1.0 / math.sqrt(HEAD_DIM)
    r = jnp.arange(BLOCK)
    tri_in = (r[:, None] > r[None, :]).astype(jnp.float32)
    outs = []
    for bi in range(nb):
        lo, hi = bi * BLOCK, (bi + 1) * BLOCK
        nk = bi + 1
        z = jnp.einsum('bhqd,bhkd->bhqk', q[:, :, lo:hi], k[:, :, :hi]).astype(jnp.float32) * scale
        mask = jnp.arange(hi)[None, :] < (lo + r)[:, None]
        log_1m = jnp.where(mask, -jax.nn.softplus(z), 0.0)
        lb = log_1m.reshape(b, h, BLOCK, nk, BLOCK)
        tail_in = jnp.einsum('bhqnj,js->bhqns', lb, tri_in)
        rb = jnp.arange(nk)
        tri_x = (rb[:, None] > rb[None, :]).astype(jnp.float32)
        tail_x = jnp.einsum('bhqm,mn->bhqn', jnp.sum(lb, axis=-1), tri_x)
        tail = (tail_in + tail_x[..., None]).reshape(b, h, BLOCK, hi)
        a = jnp.where(mask, jnp.exp(jax.nn.log_sigmoid(z) + tail), 0.0)
        outs.append(jnp.einsum('bhqk,bhkd->bhqd', a.astype(v.dtype), v[:, :, :hi]))
    return merge_heads(jnp.concatenate(outs, axis=2))


def forgetting_attention(q, k, v, c):
    s_len = q.shape[2]
    nb = s_len // BLOCK
    scale = 1.0 / math.sqrt(HEAD_DIM)
    r = jnp.arange(BLOCK)
    outs = []
    for bi in range(nb):
        lo, hi = bi * BLOCK, (bi + 1) * BLOCK
        z = jnp.einsum('bhqd,bhkd->bhqk', q[:, :, lo:hi], k[:, :, :hi]).astype(jnp.float32) * scale
        z = z + c[:, :, lo:hi, None] - c[:, :, None, :hi]
        mask = jnp.arange(hi)[None, :] <= (lo + r)[:, None]
        w = jax.nn.softmax(jnp.where(mask, z, -jnp.inf), axis=-1)
        outs.append(jnp.einsum('bhqk,bhkd->bhqd', w.astype(v.dtype), v[:, :, :hi]))
    return merge_heads(jnp.concatenate(outs, axis=2))


def swiglu(x, w_gu, w_down):
    g, u = jnp.split(x @ w_gu, 2, axis=-1)
    return (jax.nn.silu(g) * u) @ w_down


def moe_swiglu(x, w_router, w_gu, w_down):
    logits = jnp.einsum('bsd,de->bse', x, w_router).astype(jnp.float32)
    top_val, top_idx = lax.top_k(logits, TOP_K)
    gates = jax.nn.softmax(top_val, axis=-1)
    gate_full = jnp.einsum('bsk,bske->bse', gates,
                           jax.nn.one_hot(top_idx, N_EXPERTS, dtype=jnp.float32))
    gate_full = gate_full.astype(x.dtype)
    out = jnp.zeros_like(x)
    for e in range(N_EXPERTS):
        out = out + gate_full[..., e:e + 1] * swiglu(x, w_gu[e], w_down[e])
    return out


def setup_inputs(seed: int = 0) -> dict:
    key = jax.random.key(seed)
    ks = jax.random.split(key, 20)
    d = D_MODEL
    f32 = jnp.float32
    sd = d ** -0.5
    beta = DEEPNORM_BETA
    nrm = lambda k, shape: jax.random.normal(k, shape, f32)

    x = nrm(ks[0], (BATCH, SEQ, d))
    p = nrm(ks[1], (DEPTH, BATCH, SEQ, PLE_DIM))
    qkv_col_scale = jnp.concatenate([jnp.ones((2 * d,), f32), jnp.full((d,), beta, f32)])
    a_w_qkv = nrm(ks[2], (N_A, d, 3 * d)) * sd * qkv_col_scale
    a_w_o = nrm(ks[3], (N_A, d, d)) * sd * beta
    kvf_col_scale = jnp.concatenate([jnp.ones((d,), f32), jnp.full((d,), beta, f32),
                                     jnp.ones((N_HEADS,), f32)])
    kvf_w = nrm(ks[4], (d, 2 * d + N_HEADS)) * sd * kvf_col_scale
    kvf_b_f = FORGET_BIAS_INIT + 0.1 * nrm(ks[5], (N_HEADS,))
    b_w_q = nrm(ks[6], (N_B, d, d)) * sd
    b_w_o = nrm(ks[7], (N_B, d, d)) * sd * beta
    ffn_w_gu = nrm(ks[8], (N_DENSE, d, 2 * D_FF)) * sd
    ffn_w_down = nrm(ks[9], (N_DENSE, D_FF, d)) * (D_FF ** -0.5) * beta
    moe_w_router = nrm(ks[10], (N_MOE, d, N_EXPERTS)) * sd
    moe_w_gu = nrm(ks[11], (N_MOE, N_EXPERTS, d, 2 * D_FF_EXPERT)) * sd
    moe_w_down = nrm(ks[12], (N_MOE, N_EXPERTS, D_FF_EXPERT, d)) * (D_FF_EXPERT ** -0.5) * beta
    ple_w_proj = nrm(ks[13], (DEPTH, PLE_DIM, d)) * (PLE_DIM ** -0.5) * beta
    ple_w_gate = nrm(ks[14], (DEPTH, d, d)) * sd
    ln_g = 1.0 + 0.02 * nrm(ks[15], (DEPTH, 2, d))
    ln_b = 0.02 * nrm(ks[16], (DEPTH, 2, d))
    return {"x": x, "p": p, "a_w_qkv": a_w_qkv, "a_w_o": a_w_o,
            "kvf_w": kvf_w, "kvf_b_f": kvf_b_f, "b_w_q": b_w_q, "b_w_o": b_w_o,
            "ffn_w_gu": ffn_w_gu, "ffn_w_down": ffn_w_down,
            "moe_w_router": moe_w_router, "moe_w_gu": moe_w_gu, "moe_w_down": moe_w_down,
            "ple_w_proj": ple_w_proj, "ple_w_gate": ple_w_gate,
            "ln_g": ln_g, "ln_b": ln_b}


def reference(x, p, a_w_qkv, a_w_o, kvf_w, kvf_b_f, b_w_q, b_w_o,
              ffn_w_gu, ffn_w_down, moe_w_router, moe_w_gu, moe_w_down,
              ple_w_proj, ple_w_gate, ln_g, ln_b):
    d = D_MODEL
    k_sh = v_sh = c_sh = None
    for i in range(DEPTH):
        if i < N_A:
            q, k, v = jnp.split(x @ a_w_qkv[i], 3, axis=-1)
            mix = stick_breaking_attention(split_heads(q), split_heads(k), split_heads(v))
            mix = mix @ a_w_o[i]
        else:
            if i == N_A:
                kvf = x @ kvf_w
                k_sh = split_heads(kvf[..., :d])
                v_sh = split_heads(kvf[..., d:2 * d])
                log_f = jax.nn.log_sigmoid(kvf[..., 2 * d:].astype(jnp.float32)
                                           + kvf_b_f.astype(jnp.float32))
                c_sh = jnp.cumsum(log_f.transpose(0, 2, 1), axis=-1)
            j = i - N_A
            q = split_heads(x @ b_w_q[j])
            mix = forgetting_attention(q, k_sh, v_sh, c_sh) @ b_w_o[j]
        x = layer_norm(DEEPNORM_ALPHA * x + mix, ln_g[i, 0], ln_b[i, 0])
        if i % 2 == 0:
            ff = swiglu(x, ffn_w_gu[i // 2], ffn_w_down[i // 2])
        else:
            ff = moe_swiglu(x, moe_w_router[i // 2], moe_w_gu[i // 2], moe_w_down[i // 2])
        x = layer_norm(DEEPNORM_ALPHA * x + ff, ln_g[i, 1], ln_b[i, 1])
        x = x + (p[i] @ ple_w_proj[i]) * jax.nn.sigmoid(x @ ple_w_gate[i])
    return x
```

```python
import functools
import math

import jax
import jax.numpy as jnp
from jax import lax
from jax.experimental import pallas as pl
from jax.experimental.pallas import tpu as pltpu

F32 = jnp.float32
BF16 = jnp.bfloat16

N_HEADS = 4
N_EXPERTS = 8
LN_EPS = 1e-5

V7X_LANES = 128
V7X_VMEM_LIMIT = 56 * 1024 * 1024

ROW_BLOCK = 512
ATTN_BLOCK = 256


def _params(*sem):
    return pltpu.CompilerParams(dimension_semantics=sem, vmem_limit_bytes=V7X_VMEM_LIMIT)


def _resident(shape, index_map):
    return pl.BlockSpec(shape, index_map, pipeline_mode=pl.Buffered(1))


def _dot(a, b):
    return jnp.dot(a, b, preferred_element_type=F32)


def _dot_nt(a, b):
    return lax.dot_general(a, b, (((1,), (1,)), ((), ())), preferred_element_type=F32)


def _split2(x):
    hi = x.astype(BF16)
    lo = (x - hi.astype(F32)).astype(BF16)
    return hi, lo


def _split3(x):
    a = x.astype(BF16)
    r = x - a.astype(F32)
    b = r.astype(BF16)
    c = (r - b.astype(F32)).astype(BF16)
    return a, b, c


def _hilo_weight(w):
    k, n = w.shape
    hi = w.astype(BF16)
    lo = (w - hi.astype(F32)).astype(BF16)
    out = jnp.zeros((k, V7X_LANES), BF16)
    out = out.at[:, :n].set(hi)
    out = out.at[:, 64:64 + n].set(lo)
    return out


def _hilo_dot(x, w2):
    xh, xl = _split2(x)
    a = _dot(xh, w2)
    b = _dot(xl, w2)
    return a + pltpu.roll(a, 64, 1) + b


def _layer_norm(y, g, b):
    mu = jnp.mean(y, axis=-1, keepdims=True)
    d = y - mu
    var = jnp.mean(d * d, axis=-1, keepdims=True)
    return d * lax.rsqrt(var + LN_EPS) * g + b


def _proj_kernel(x_ref, w_ref, o_ref):
    o_ref[...] = _dot(x_ref[...].astype(BF16), w_ref[...]).astype(o_ref.dtype)


def _proj(x, w):
    t, k = x.shape
    n = w.shape[1]
    return pl.pallas_call(
        _proj_kernel,
        out_shape=jax.ShapeDtypeStruct((t, n), BF16),
        grid=(t // ROW_BLOCK,),
        in_specs=[pl.BlockSpec((ROW_BLOCK, k), lambda i: (i, 0)),
                  _resident((k, n), lambda i: (0, 0))],
        out_specs=pl.BlockSpec((ROW_BLOCK, n), lambda i: (i, 0)),
        compiler_params=_params("parallel"),
        name="proj",
    )(x, w)


def _proj_kvf_kernel(x_ref, w_ref, wf_ref, o_ref, f_ref):
    x = x_ref[...]
    o_ref[...] = _dot(x.astype(BF16), w_ref[...]).astype(o_ref.dtype)
    f_ref[...] = _hilo_dot(x, wf_ref[...])


def _proj_kvf(x, w, wf2):
    t, k = x.shape
    n = w.shape[1]
    return pl.pallas_call(
        _proj_kvf_kernel,
        out_shape=(jax.ShapeDtypeStruct((t, n), BF16),
                   jax.ShapeDtypeStruct((t, V7X_LANES), F32)),
        grid=(t // ROW_BLOCK,),
        in_specs=[pl.BlockSpec((ROW_BLOCK, k), lambda i: (i, 0)),
                  _resident((k, n), lambda i: (0, 0)),
                  _resident((k, V7X_LANES), lambda i: (0, 0))],
        out_specs=(pl.BlockSpec((ROW_BLOCK, n), lambda i: (i, 0)),
                   pl.BlockSpec((ROW_BLOCK, V7X_LANES), lambda i: (i, 0))),
        compiler_params=_params("parallel"),
        name="proj_kvf",
    )(x, w, wf2)


def _cumsum_kernel(v_ref, bias_ref, c_ref, *, rows_per_seq):
    z = v_ref[...] + bias_ref[...]
    a = jnp.minimum(z, 0.0) - jnp.log(1.0 + jnp.exp(-jnp.abs(z)))
    r = a.shape[0]
    j = lax.broadcasted_iota(jnp.int32, (V7X_LANES, V7X_LANES), 0)
    s = lax.broadcasted_iota(jnp.int32, (V7X_LANES, V7X_LANES), 1)
    upper = jnp.where(j <= s, 1.0, 0.0).astype(BF16)
    a1, a2, a3 = _split3(a)
    within = _dot(a1, upper) + _dot(a2, upper) + _dot(a3, upper)
    ri = lax.broadcasted_iota(jnp.int32, (r, r), 0)
    rj = lax.broadcasted_iota(jnp.int32, (r, r), 1)
    same = (ri // rows_per_seq) == (rj // rows_per_seq)
    before = jnp.where(same & (rj < ri), 1.0, 0.0).astype(BF16)
    w1, w2, w3 = _split3(within)
    offs = _dot(before, w1) + _dot(before, w2) + _dot(before, w3)
    c_ref[...] = within + offs[:, V7X_LANES - 1:V7X_LANES]


def _log_forget_cumsum(f_logit, bias_f, bsz, seq):
    rows_per_seq = seq // V7X_LANES
    fl = f_logit.reshape(bsz, seq, V7X_LANES)[:, :, :N_HEADS].transpose(0, 2, 1)
    fl = fl.reshape(bsz * N_HEADS * rows_per_seq, V7X_LANES)
    bias = jnp.broadcast_to(bias_f.astype(F32)[None, :, None, None],
                            (bsz, N_HEADS, rows_per_seq, 1)).reshape(-1, 1)
    c = pl.pallas_call(
        functools.partial(_cumsum_kernel, rows_per_seq=rows_per_seq),
        out_shape=jax.ShapeDtypeStruct(fl.shape, F32),
        compiler_params=_params(),
        name="forget_cumsum",
    )(fl, bias)
    return c.reshape(bsz, N_HEADS, seq)


def _softplus(z):
    return jnp.maximum(z, 0.0) + jnp.log(1.0 + jnp.exp(-jnp.abs(z)))


def _sb_attn_kernel(q_ref, k_ref, v_ref, o_ref, acc_ref, carry_ref, *, blk):
    qi = pl.program_id(2)
    q = q_ref[...]
    row = lax.broadcasted_iota(jnp.int32, (blk, blk), 0)
    col = lax.broadcasted_iota(jnp.int32, (blk, blk), 1)
    suffix = jnp.where(row >= col, 1.0, 0.0).astype(BF16)
    suffix2 = jnp.concatenate([suffix, suffix], axis=0)

    def key_block(j, causal_mask):
        start = pl.multiple_of(j * blk, blk)
        k = k_ref[pl.ds(start, blk), :]
        v = v_ref[pl.ds(start, blk), :]
        z = _dot_nt(q, k)
        sp = _softplus(z)
        if causal_mask is not None:
            sp = jnp.where(causal_mask, sp, 0.0)
        hi, lo = _split2(sp)
        r = _dot(jnp.concatenate([hi, lo], axis=1), suffix2)
        p = jnp.exp(z - r - carry_ref[...])
        if causal_mask is not None:
            p = jnp.where(causal_mask, p, 0.0)
        acc_ref[...] += _dot(p.astype(BF16), v)
        carry_ref[...] += r[:, 0:1]

    acc_ref[...] = jnp.zeros_like(acc_ref)
    carry_ref[...] = jnp.zeros_like(carry_ref)
    key_block(qi, col < row)

    def body(i, c):
        key_block(qi - 1 - i, None)
        return c

    lax.fori_loop(0, qi, body, 0)
    o_ref[...] = acc_ref[...].astype(o_ref.dtype)


def _sb_attention(qkv, d_model):
    b, s, _ = qkv.shape
    dh = d_model // N_HEADS
    blk = ATTN_BLOCK
    return pl.pallas_call(
        functools.partial(_sb_attn_kernel, blk=blk),
        out_shape=jax.ShapeDtypeStruct((b, s, d_model), BF16),
        grid=(b, N_HEADS, s // blk),
        in_specs=[pl.BlockSpec((None, blk, dh), lambda bi, h, qi: (bi, qi, h)),
                  _resident((None, s, dh), lambda bi, h, qi: (bi, 0, N_HEADS + h)),
                  _resident((None, s, dh), lambda bi, h, qi: (bi, 0, 2 * N_HEADS + h))],
        out_specs=pl.BlockSpec((None, blk, dh), lambda bi, h, qi: (bi, qi, h)),
        scratch_shapes=[pltpu.VMEM((blk, dh), F32), pltpu.VMEM((blk, 1), F32)],
        compiler_params=_params("parallel", "parallel", "arbitrary"),
        name="sb_attention",
    )(qkv, qkv, qkv)


def _fox_attn_kernel(q_ref, k_ref, v_ref, c_ref, o_ref, acc_ref, m_ref, l_ref, *, blk):
    qi = pl.program_id(2)
    q = q_ref[...]
    row = lax.broadcasted_iota(jnp.int32, (blk, blk), 0)
    col = lax.broadcasted_iota(jnp.int32, (blk, blk), 1)
    c_first = c_ref[pl.ds(qi, 1), :][:, 0:1]

    def key_block(j, causal_mask):
        start = pl.multiple_of(j * blk, blk)
        k = k_ref[pl.ds(start, blk), :]
        v = v_ref[pl.ds(start, blk), :]
        bias = c_first - c_ref[pl.ds(j, 1), :]
        z = _dot_nt(q, k) + bias
        if causal_mask is not None:
            z = jnp.where(causal_mask, z, -jnp.inf)
        m_old = m_ref[...]
        m_new = jnp.maximum(m_old, jnp.max(z, axis=-1, keepdims=True))
        p = jnp.exp(z - m_new)
        alpha = jnp.exp(m_old - m_new)
        l_ref[...] = alpha * l_ref[...] + jnp.sum(p, axis=-1, keepdims=True)
        acc_ref[...] = alpha * acc_ref[...] + _dot(p.astype(BF16), v)
        m_ref[...] = m_new

    acc_ref[...] = jnp.zeros_like(acc_ref)
    l_ref[...] = jnp.zeros_like(l_ref)
    m_ref[...] = jnp.full_like(m_ref, -jnp.inf)
    key_block(qi, col <= row)

    def body(i, c):
        key_block(qi - 1 - i, None)
        return c

    lax.fori_loop(0, qi, body, 0)
    o_ref[...] = (acc_ref[...] / l_ref[...]).astype(o_ref.dtype)


def _fox_attention(q, kv, c, d_model):
    b, s, _ = q.shape
    dh = d_model // N_HEADS
    blk = ATTN_BLOCK
    return pl.pallas_call(
        functools.partial(_fox_attn_kernel, blk=blk),
        out_shape=jax.ShapeDtypeStruct((b, s, d_model), BF16),
        grid=(b, N_HEADS, s // blk),
        in_specs=[pl.BlockSpec((None, blk, dh), lambda bi, h, qi: (bi, qi, h)),
                  _resident((None, s, dh), lambda bi, h, qi: (bi, 0, h)),
                  _resident((None, s, dh), lambda bi, h, qi: (bi, 0, N_HEADS + h)),
                  _resident((None, None, s // blk, blk), lambda bi, h, qi: (bi, h, 0, 0))],
        out_specs=pl.BlockSpec((None, blk, dh), lambda bi, h, qi: (bi, qi, h)),
        scratch_shapes=[pltpu.VMEM((blk, dh), F32), pltpu.VMEM((blk, 1), F32), pltpu.VMEM((blk, 1), F32)],
        compiler_params=_params("parallel", "parallel", "arbitrary"),
        name="fox_attention",
    )(q, kv, kv, c)


def _out_ln_kernel(x_ref, a_ref, w_ref, g_ref, b_ref, o_ref, *, alpha):
    y = alpha * x_ref[...] + _dot(a_ref[...], w_ref[...])
    o_ref[...] = _layer_norm(y, g_ref[...], b_ref[...])


def _out_ln(x, a, w, g, b, alpha):
    t, d = x.shape
    return pl.pallas_call(
        functools.partial(_out_ln_kernel, alpha=alpha),
        out_shape=jax.ShapeDtypeStruct((t, d), F32),
        grid=(t // ROW_BLOCK,),
        in_specs=[pl.BlockSpec((ROW_BLOCK, d), lambda i: (i, 0)),
                  pl.BlockSpec((ROW_BLOCK, d), lambda i: (i, 0)),
                  _resident((d, d), lambda i: (0, 0)),
                  _resident((1, d), lambda i: (0, 0)),
                  _resident((1, d), lambda i: (0, 0))],
        out_specs=pl.BlockSpec((ROW_BLOCK, d), lambda i: (i, 0)),
        compiler_params=_params("parallel"),
        name="out_ln",
    )(x, a, w, g, b)


def _silu(g):
    return g / (1.0 + jnp.exp(-g))


def _ln_ple(x, ff, g, b, p, w_proj, w_gate, alpha):
    y = _layer_norm(alpha * x + ff, g, b)
    gate = 1.0 / (1.0 + jnp.exp(-_dot(y.astype(BF16), w_gate)))
    return y + _dot(p.astype(BF16), w_proj) * gate


def _dense_ffn_kernel(x_ref, wgu_ref, wd_ref, g_ref, b_ref, p_ref, wp_ref, wg_ref, o_ref, *, alpha, d_ff):
    x = x_ref[...]
    h = _dot(x.astype(BF16), wgu_ref[...])
    act = (_silu(h[:, :d_ff]) * h[:, d_ff:]).astype(BF16)
    ff = _dot(act, wd_ref[...])
    o_ref[...] = _ln_ple(x, ff, g_ref[...], b_ref[...], p_ref[...], wp_ref[...], wg_ref[...], alpha)


DENSE_ROW_BLOCK = 256


def _dense_ffn(x, w_gu, w_down, g, b, p, w_proj, w_gate, alpha):
    t, d = x.shape
    d_ff = w_down.shape[0]
    pd = p.shape[1]
    tm = DENSE_ROW_BLOCK
    const = lambda i: (0, 0)
    return pl.pallas_call(
        functools.partial(_dense_ffn_kernel, alpha=alpha, d_ff=d_ff),
        out_shape=jax.ShapeDtypeStruct((t, d), F32),
        grid=(t // tm,),
        in_specs=[pl.BlockSpec((tm, d), lambda i: (i, 0)),
                  _resident((d, 2 * d_ff), const),
                  _resident((d_ff, d), const),
                  _resident((1, d), const),
                  _resident((1, d), const),
                  pl.BlockSpec((tm, pd), lambda i: (i, 0)),
                  _resident((pd, d), const),
                  _resident((d, d), const)],
        out_specs=pl.BlockSpec((tm, d), lambda i: (i, 0)),
        compiler_params=_params("parallel"),
        name="dense_ffn",
    )(x, w_gu, w_down, g, b, p, w_proj, w_gate)


def _top2_gates(logits):
    lane = lax.broadcasted_iota(jnp.int32, logits.shape, 1)
    v1 = jnp.max(logits, axis=-1, keepdims=True)
    i1 = jnp.min(jnp.where(logits == v1, lane, V7X_LANES), axis=-1, keepdims=True)
    rest = jnp.where(lane == i1, -jnp.inf, logits)
    v2 = jnp.max(rest, axis=-1, keepdims=True)
    i2 = jnp.min(jnp.where(rest == v2, lane, V7X_LANES), axis=-1, keepdims=True)
    e2 = jnp.exp(v2 - v1)
    g1 = 1.0 / (1.0 + e2)
    g2 = e2 / (1.0 + e2)
    return i1, g1, i2, g2


def _moe_ffn_kernel(x_ref, wr_ref, wg_ref, wu_ref, wd_ref, g_ref, b_ref, p_ref, wp_ref, wgate_ref, o_ref,
                    acc_ref, route_ref, *, alpha, f_exp):
    pr = pl.program_id(1)
    x = x_ref[...]
    tm = x.shape[0]

    @pl.when(pr == 0)
    def _():
        lane = lax.broadcasted_iota(jnp.int32, (tm, V7X_LANES), 1)
        logits = jnp.where(lane < N_EXPERTS, _hilo_dot(x, wr_ref[...]), -jnp.inf)
        i1, g1, i2, g2 = _top2_gates(logits)
        route_ref[...] = jnp.where(lane == i1, g1, 0.0) + jnp.where(lane == i2, g2, 0.0)
        acc_ref[...] = jnp.zeros_like(acc_ref)

    xb = x.astype(BF16)
    act = _silu(_dot(xb, wg_ref[...])) * _dot(xb, wu_ref[...])
    lane = lax.broadcasted_iota(jnp.int32, (tm, V7X_LANES), 1)
    gates = route_ref[...]
    gate_a = jnp.sum(jnp.where(lane == 2 * pr, gates, 0.0), axis=-1, keepdims=True)
    gate_b = jnp.sum(jnp.where(lane == 2 * pr + 1, gates, 0.0), axis=-1, keepdims=True)
    lo = (f_exp // V7X_LANES) * V7X_LANES
    hi = lo + V7X_LANES
    mixed = jnp.where(lane < f_exp - lo, gate_a, gate_b)
    act = jnp.concatenate([act[:, :lo] * gate_a, act[:, lo:hi] * mixed, act[:, hi:] * gate_b], axis=1)
    acc_ref[...] += _dot(act.astype(BF16), wd_ref[...])

    @pl.when(pr == pl.num_programs(1) - 1)
    def _():
        o_ref[...] = _ln_ple(x, acc_ref[...], g_ref[...], b_ref[...], p_ref[...], wp_ref[...], wgate_ref[...],
                             alpha)


def _moe_ffn(x, w_router2, w_g, w_u, w_d, g, b, p, w_proj, w_gate, alpha):
    t, d = x.shape
    f_exp = w_d.shape[0] // N_EXPERTS
    pd = p.shape[1]
    tm = ROW_BLOCK
    const = lambda i, pr: (0, 0)
    return pl.pallas_call(
        functools.partial(_moe_ffn_kernel, alpha=alpha, f_exp=f_exp),
        out_shape=jax.ShapeDtypeStruct((t, d), F32),
        grid=(t // tm, N_EXPERTS // 2),
        in_specs=[pl.BlockSpec((tm, d), lambda i, pr: (i, 0)),
                  _resident((d, V7X_LANES), const),
                  pl.BlockSpec((d, 2 * f_exp), lambda i, pr: (0, pr)),
                  pl.BlockSpec((d, 2 * f_exp), lambda i, pr: (0, pr)),
                  pl.BlockSpec((2 * f_exp, d), lambda i, pr: (pr, 0)),
                  _resident((1, d), const),
                  _resident((1, d), const),
                  pl.BlockSpec((tm, pd), lambda i, pr: (i, 0)),
                  _resident((pd, d), const),
                  _resident((d, d), const)],
        out_specs=pl.BlockSpec((tm, d), lambda i, pr: (i, 0)),
        scratch_shapes=[pltpu.VMEM((tm, d), F32), pltpu.VMEM((tm, V7X_LANES), F32)],
        compiler_params=_params("parallel", "arbitrary"),
        name="moe_ffn",
    )(x, w_router2, w_g, w_u, w_d, g, b, p, w_proj, w_gate)


def kernel(x, p, a_w_qkv, a_w_o, kvf_w, kvf_b_f, b_w_q, b_w_o, ffn_w_gu, ffn_w_down, moe_w_router, moe_w_gu,
           moe_w_down, ple_w_proj, ple_w_gate, ln_g, ln_b):
    bsz, seq, d = x.shape
    depth = p.shape[0]
    n_a = a_w_qkv.shape[0]
    t = bsz * seq
    dh = d // N_HEADS
    alpha = (2.0 * depth) ** 0.25
    scale = 1.0 / math.sqrt(dh)
    f_exp = moe_w_down.shape[2]

    xf = x.reshape(t, d)
    pf = p.reshape(depth, t, p.shape[-1])

    c_sh = kv_sh = None
    for i in range(depth):
        g1, b1 = ln_g[i, 0][None, :], ln_b[i, 0][None, :]
        g2, b2 = ln_g[i, 1][None, :], ln_b[i, 1][None, :]
        if i < n_a:
            w_qkv = jnp.concatenate([a_w_qkv[i][:, :d] * scale, a_w_qkv[i][:, d:]], axis=1).astype(BF16)
            qkv = _proj(xf, w_qkv).reshape(bsz, seq, 3 * d)
            mix = _sb_attention(qkv, d).reshape(t, d)
            w_o = a_w_o[i].astype(BF16)
        else:
            if i == n_a:
                kv, f_logit = _proj_kvf(xf, kvf_w[:, :2 * d].astype(BF16), _hilo_weight(kvf_w[:, 2 * d:]))
                kv_sh = kv.reshape(bsz, seq, 2 * d)
                c_sh = _log_forget_cumsum(f_logit, kvf_b_f, bsz, seq)
                c_sh = c_sh.reshape(bsz, N_HEADS, seq // ATTN_BLOCK, ATTN_BLOCK)
            j = i - n_a
            q = _proj(xf, (b_w_q[j] * scale).astype(BF16)).reshape(bsz, seq, d)
            mix = _fox_attention(q, kv_sh, c_sh, d).reshape(t, d)
            w_o = b_w_o[j].astype(BF16)
        xf = _out_ln(xf, mix, w_o, g1, b1, alpha)

        w_proj = ple_w_proj[i].astype(BF16)
        w_gate = ple_w_gate[i].astype(BF16)
        if i % 2 == 0:
            xf = _dense_ffn(xf, ffn_w_gu[i // 2].astype(BF16), ffn_w_down[i // 2].astype(BF16), g2, b2,
                            pf[i], w_proj, w_gate, alpha)
        else:
            m = i // 2
            w_g = moe_w_gu[m][:, :, :f_exp].transpose(1, 0, 2).reshape(d, N_EXPERTS * f_exp).astype(BF16)
            w_u = moe_w_gu[m][:, :, f_exp:].transpose(1, 0, 2).reshape(d, N_EXPERTS * f_exp).astype(BF16)
            w_d = moe_w_down[m].reshape(N_EXPERTS * f_exp, d).astype(BF16)
            xf = _moe_ffn(xf, _hilo_weight(moe_w_router[m]), w_g, w_u, w_d, g2, b2, pf[i], w_proj, w_gate,
                          alpha)
    return xf.reshape(bsz, seq, d)
```

```python
import functools
import math

import jax
import jax.numpy as jnp
from jax import lax
from jax.experimental import pallas as pl
from jax.experimental.pallas import tpu as pltpu

F32 = jnp.float32
BF16 = jnp.bfloat16

N_HEADS = 4
N_EXPERTS = 8
LN_EPS = 1e-5

V7X_LANES = 128
V7X_VMEM_LIMIT = 56 * 1024 * 1024

UNDERFLOW_EXP = 110.0

ROW_BLOCK = 512
ATTN_BLOCK = 256


def _params(*sem):
    return pltpu.CompilerParams(dimension_semantics=sem, vmem_limit_bytes=V7X_VMEM_LIMIT)


def _resident(shape, index_map):
    return pl.BlockSpec(shape, index_map, pipeline_mode=pl.Buffered(1))


def _dot(a, b):
    return jnp.dot(a, b, preferred_element_type=F32)


def _dot_nt(a, b):
    return lax.dot_general(a, b, (((1,), (1,)), ((), ())), preferred_element_type=F32)


def _split2(x):
    hi = x.astype(BF16)
    lo = (x - hi.astype(F32)).astype(BF16)
    return hi, lo


def _split3(x):
    a = x.astype(BF16)
    r = x - a.astype(F32)
    b = r.astype(BF16)
    c = (r - b.astype(F32)).astype(BF16)
    return a, b, c


def _hilo_weight(w):
    k, n = w.shape
    hi = w.astype(BF16)
    lo = (w - hi.astype(F32)).astype(BF16)
    out = jnp.zeros((k, V7X_LANES), BF16)
    out = out.at[:, :n].set(hi)
    out = out.at[:, 64:64 + n].set(lo)
    return out


def _hilo_dot(x, w2):
    xh, xl = _split2(x)
    a = _dot(xh, w2)
    b = _dot(xl, w2)
    return a + pltpu.roll(a, 64, 1) + b


def _layer_norm(y, g, b):
    mu = jnp.mean(y, axis=-1, keepdims=True)
    d = y - mu
    var = jnp.mean(d * d, axis=-1, keepdims=True)
    return d * lax.rsqrt(var + LN_EPS) * g + b


def _proj_kernel(x_ref, w_ref, o_ref):
    o_ref[...] = _dot(x_ref[...].astype(BF16), w_ref[...]).astype(o_ref.dtype)


def _proj(x, w):
    t, k = x.shape
    n = w.shape[1]
    return pl.pallas_call(
        _proj_kernel,
        out_shape=jax.ShapeDtypeStruct((t, n), BF16),
        grid=(t // ROW_BLOCK,),
        in_specs=[pl.BlockSpec((ROW_BLOCK, k), lambda i: (i, 0)),
                  _resident((k, n), lambda i: (0, 0))],
        out_specs=pl.BlockSpec((ROW_BLOCK, n), lambda i: (i, 0)),
        compiler_params=_params("parallel"),
        name="proj",
    )(x, w)


def _proj_kvf_kernel(x_ref, w_ref, wf_ref, o_ref, f_ref):
    x = x_ref[...]
    o_ref[...] = _dot(x.astype(BF16), w_ref[...]).astype(o_ref.dtype)
    f_ref[...] = _hilo_dot(x, wf_ref[...])


def _proj_kvf(x, w, wf2):
    t, k = x.shape
    n = w.shape[1]
    return pl.pallas_call(
        _proj_kvf_kernel,
        out_shape=(jax.ShapeDtypeStruct((t, n), BF16),
                   jax.ShapeDtypeStruct((t, V7X_LANES), F32)),
        grid=(t // ROW_BLOCK,),
        in_specs=[pl.BlockSpec((ROW_BLOCK, k), lambda i: (i, 0)),
                  _resident((k, n), lambda i: (0, 0)),
                  _resident((k, V7X_LANES), lambda i: (0, 0))],
        out_specs=(pl.BlockSpec((ROW_BLOCK, n), lambda i: (i, 0)),
                   pl.BlockSpec((ROW_BLOCK, V7X_LANES), lambda i: (i, 0))),
        compiler_params=_params("parallel"),
        name="proj_kvf",
    )(x, w, wf2)


def _cumsum_kernel(v_ref, bias_ref, c_ref, *, rows_per_seq):
    z = v_ref[...] + bias_ref[...]
    a = jnp.minimum(z, 0.0) - jnp.log(1.0 + jnp.exp(-jnp.abs(z)))
    r = a.shape[0]
    j = lax.broadcasted_iota(jnp.int32, (V7X_LANES, V7X_LANES), 0)
    s = lax.broadcasted_iota(jnp.int32, (V7X_LANES, V7X_LANES), 1)
    upper = jnp.where(j <= s, 1.0, 0.0).astype(BF16)
    a1, a2, a3 = _split3(a)
    within = _dot(a1, upper) + _dot(a2, upper) + _dot(a3, upper)
    ri = lax.broadcasted_iota(jnp.int32, (r, r), 0)
    rj = lax.broadcasted_iota(jnp.int32, (r, r), 1)
    same = (ri // rows_per_seq) == (rj // rows_per_seq)
    before = jnp.where(same & (rj < ri), 1.0, 0.0).astype(BF16)
    w1, w2, w3 = _split3(within)
    offs = _dot(before, w1) + _dot(before, w2) + _dot(before, w3)
    c_ref[...] = within + offs[:, V7X_LANES - 1:V7X_LANES]


def _log_forget_cumsum(f_logit, bias_f, bsz, seq):
    rows_per_seq = seq // V7X_LANES
    fl = f_logit.reshape(bsz, seq, V7X_LANES)[:, :, :N_HEADS].transpose(0, 2, 1)
    fl = fl.reshape(bsz * N_HEADS * rows_per_seq, V7X_LANES)
    bias = jnp.broadcast_to(bias_f.astype(F32)[None, :, None, None],
                            (bsz, N_HEADS, rows_per_seq, 1)).reshape(-1, 1)
    c = pl.pallas_call(
        functools.partial(_cumsum_kernel, rows_per_seq=rows_per_seq),
        out_shape=jax.ShapeDtypeStruct(fl.shape, F32),
        compiler_params=_params(),
        name="forget_cumsum",
    )(fl, bias)
    return c.reshape(bsz, N_HEADS, seq)


def _softplus(z):
    return jnp.maximum(z, 0.0) + jnp.log(1.0 + jnp.exp(-jnp.abs(z)))


def _sb_attn_kernel(q_ref, k_ref, v_ref, o_ref, acc_ref, carry_ref, *, blk):
    qi = pl.program_id(2)
    q = q_ref[...]
    row = lax.broadcasted_iota(jnp.int32, (blk, blk), 0)
    col = lax.broadcasted_iota(jnp.int32, (blk, blk), 1)
    suffix = jnp.where(row >= col, 1.0, 0.0).astype(BF16)
    suffix2 = jnp.concatenate([suffix, suffix], axis=0)

    def key_block(j, causal_mask):
        start = pl.multiple_of(j * blk, blk)
        k = k_ref[pl.ds(start, blk), :]
        v = v_ref[pl.ds(start, blk), :]
        z = _dot_nt(q, k)
        sp = _softplus(z)
        if causal_mask is not None:
            sp = jnp.where(causal_mask, sp, 0.0)
        hi, lo = _split2(sp)
        r = _dot(jnp.concatenate([hi, lo], axis=1), suffix2)
        p = jnp.exp(z - r - carry_ref[...])
        if causal_mask is not None:
            p = jnp.where(causal_mask, p, 0.0)
        acc_ref[...] += _dot(p.astype(BF16), v)
        carry_ref[...] += r[:, 0:1]

    acc_ref[...] = jnp.zeros_like(acc_ref)
    carry_ref[...] = jnp.zeros_like(carry_ref)
    key_block(qi, col < row)

    def live():
        return jnp.min(carry_ref[...]) < UNDERFLOW_EXP

    def body(state):
        j, _ = state
        key_block(j, None)
        return j - 1, live()

    lax.while_loop(lambda st: (st[0] >= 0) & st[1], body, (qi - 1, live()))
    o_ref[...] = acc_ref[...].astype(o_ref.dtype)


def _sb_attention(qkv, d_model):
    b, s, _ = qkv.shape
    dh = d_model // N_HEADS
    blk = ATTN_BLOCK
    return pl.pallas_call(
        functools.partial(_sb_attn_kernel, blk=blk),
        out_shape=jax.ShapeDtypeStruct((b, s, d_model), BF16),
        grid=(b, N_HEADS, s // blk),
        in_specs=[pl.BlockSpec((None, blk, dh), lambda bi, h, qi: (bi, qi, h)),
                  _resident((None, s, dh), lambda bi, h, qi: (bi, 0, N_HEADS + h)),
                  _resident((None, s, dh), lambda bi, h, qi: (bi, 0, 2 * N_HEADS + h))],
        out_specs=pl.BlockSpec((None, blk, dh), lambda bi, h, qi: (bi, qi, h)),
        scratch_shapes=[pltpu.VMEM((blk, dh), F32), pltpu.VMEM((blk, 1), F32)],
        compiler_params=_params("parallel", "parallel", "arbitrary"),
        name="sb_attention",
    )(qkv, qkv, qkv)


def _row_norm_max(x):
    xf = x.astype(F32)
    return jnp.max(jnp.sqrt(jnp.sum(xf * xf, axis=-1, keepdims=True)))


def _fox_attn_kernel(cs_ref, ce_ref, q_ref, k_ref, v_ref, c_ref, o_ref, acc_ref, m_ref, l_ref, kn_ref, *, blk):
    bi, h, qi = pl.program_id(0), pl.program_id(1), pl.program_id(2)
    q = q_ref[...]
    row = lax.broadcasted_iota(jnp.int32, (blk, blk), 0)
    col = lax.broadcasted_iota(jnp.int32, (blk, blk), 1)

    @pl.when(qi == 0)
    def _():
        def chunk(i, best):
            return jnp.maximum(best, _row_norm_max(k_ref[pl.ds(pl.multiple_of(i * blk, blk), blk), :]))
        kn_ref[0] = lax.fori_loop(0, k_ref.shape[0] // blk, chunk, jnp.float32(0.0))

    qk_bound = 1.01 * _row_norm_max(q) * kn_ref[0]
    c_first = cs_ref[bi, h, qi]

    def key_block(j, causal_mask):
        start = pl.multiple_of(j * blk, blk)
        k = k_ref[pl.ds(start, blk), :]
        v = v_ref[pl.ds(start, blk), :]
        bias = c_first - c_ref[pl.ds(j, 1), :]
        z = _dot_nt(q, k) + bias
        if causal_mask is not None:
            z = jnp.where(causal_mask, z, -jnp.inf)
        m_old = m_ref[...]
        m_new = jnp.maximum(m_old, jnp.max(z, axis=-1, keepdims=True))
        p = jnp.exp(z - m_new)
        alpha = jnp.exp(m_old - m_new)
        l_ref[...] = alpha * l_ref[...] + jnp.sum(p, axis=-1, keepdims=True)
        acc_ref[...] = alpha * acc_ref[...] + _dot(p.astype(BF16), v)
        m_ref[...] = m_new

    acc_ref[...] = jnp.zeros_like(acc_ref)
    l_ref[...] = jnp.zeros_like(l_ref)
    m_ref[...] = jnp.full_like(m_ref, -jnp.inf)
    key_block(qi, col <= row)

    def live(j):
        bound = qk_bound + (c_first - ce_ref[bi, h, jnp.maximum(j, 0)])
        return bound > jnp.min(m_ref[...]) - UNDERFLOW_EXP

    def body(state):
        j, _ = state
        key_block(j, None)
        return j - 1, live(j - 1)

    lax.while_loop(lambda st: (st[0] >= 0) & st[1], body, (qi - 1, live(qi - 1)))
    o_ref[...] = (acc_ref[...] / l_ref[...]).astype(o_ref.dtype)


def _fox_attention(q, kv, c, d_model):
    b, s, _ = q.shape
    dh = d_model // N_HEADS
    blk = ATTN_BLOCK
    return pl.pallas_call(
        functools.partial(_fox_attn_kernel, blk=blk),
        out_shape=jax.ShapeDtypeStruct((b, s, d_model), BF16),
        grid=(b, N_HEADS, s // blk),
        in_specs=[pl.BlockSpec(memory_space=pltpu.SMEM),
                  pl.BlockSpec(memory_space=pltpu.SMEM),
                  pl.BlockSpec((None, blk, dh), lambda bi, h, qi: (bi, qi, h)),
                  _resident((None, s, dh), lambda bi, h, qi: (bi, 0, h)),
                  _resident((None, s, dh), lambda bi, h, qi: (bi, 0, N_HEADS + h)),
                  _resident((None, None, s // blk, blk), lambda bi, h, qi: (bi, h, 0, 0))],
        out_specs=pl.BlockSpec((None, blk, dh), lambda bi, h, qi: (bi, qi, h)),
        scratch_shapes=[pltpu.VMEM((blk, dh), F32), pltpu.VMEM((blk, 1), F32), pltpu.VMEM((blk, 1), F32),
                        pltpu.SMEM((1,), F32)],
        compiler_params=_params("parallel", "parallel", "arbitrary"),
        name="fox_attention",
    )(c[..., 0], c[..., -1], q, kv, kv, c)


def _out_ln_kernel(x_ref, a_ref, w_ref, g_ref, b_ref, o_ref, *, alpha):
    y = alpha * x_ref[...] + _dot(a_ref[...], w_ref[...])
    o_ref[...] = _layer_norm(y, g_ref[...], b_ref[...])


def _out_ln(x, a, w, g, b, alpha):
    t, d = x.shape
    return pl.pallas_call(
        functools.partial(_out_ln_kernel, alpha=alpha),
        out_shape=jax.ShapeDtypeStruct((t, d), F32),
        grid=(t // ROW_BLOCK,),
        in_specs=[pl.BlockSpec((ROW_BLOCK, d), lambda i: (i, 0)),
                  pl.BlockSpec((ROW_BLOCK, d), lambda i: (i, 0)),
                  _resident((d, d), lambda i: (0, 0)),
                  _resident((1, d), lambda i: (0, 0)),
                  _resident((1, d), lambda i: (0, 0))],
        out_specs=pl.BlockSpec((ROW_BLOCK, d), lambda i: (i, 0)),
        compiler_params=_params("parallel"),
        name="out_ln",
    )(x, a, w, g, b)


def _silu(g):
    return g / (1.0 + jnp.exp(-g))


def _ln_ple(x, ff, g, b, p, w_proj, w_gate, alpha):
    y = _layer_norm(alpha * x + ff, g, b)
    gate = 1.0 / (1.0 + jnp.exp(-_dot(y.astype(BF16), w_gate)))
    return y + _dot(p.astype(BF16), w_proj) * gate


def _dense_ffn_kernel(x_ref, wgu_ref, wd_ref, g_ref, b_ref, p_ref, wp_ref, wg_ref, o_ref, *, alpha, d_ff):
    x = x_ref[...]
    h = _dot(x.astype(BF16), wgu_ref[...])
    act = (_silu(h[:, :d_ff]) * h[:, d_ff:]).astype(BF16)
    ff = _dot(act, wd_ref[...])
    o_ref[...] = _ln_ple(x, ff, g_ref[...], b_ref[...], p_ref[...], wp_ref[...], wg_ref[...], alpha)


DENSE_ROW_BLOCK = 256


def _dense_ffn(x, w_gu, w_down, g, b, p, w_proj, w_gate, alpha):
    t, d = x.shape
    d_ff = w_down.shape[0]
    pd = p.shape[1]
    tm = DENSE_ROW_BLOCK
    const = lambda i: (0, 0)
    return pl.pallas_call(
        functools.partial(_dense_ffn_kernel, alpha=alpha, d_ff=d_ff),
        out_shape=jax.ShapeDtypeStruct((t, d), F32),
        grid=(t // tm,),
        in_specs=[pl.BlockSpec((tm, d), lambda i: (i, 0)),
                  _resident((d, 2 * d_ff), const),
                  _resident((d_ff, d), const),
                  _resident((1, d), const),
                  _resident((1, d), const),
                  pl.BlockSpec((tm, pd), lambda i: (i, 0)),
                  _resident((pd, d), const),
                  _resident((d, d), const)],
        out_specs=pl.BlockSpec((tm, d), lambda i: (i, 0)),
        compiler_params=_params("parallel"),
        name="dense_ffn",
    )(x, w_gu, w_down, g, b, p, w_proj, w_gate)


def _top2_gates(logits):
    lane = lax.broadcasted_iota(jnp.int32, logits.shape, 1)
    v1 = jnp.max(logits, axis=-1, keepdims=True)
    i1 = jnp.min(jnp.where(logits == v1, lane, V7X_LANES), axis=-1, keepdims=True)
    rest = jnp.where(lane == i1, -jnp.inf, logits)
    v2 = jnp.max(rest, axis=-1, keepdims=True)
    i2 = jnp.min(jnp.where(rest == v2, lane, V7X_LANES), axis=-1, keepdims=True)
    e2 = jnp.exp(v2 - v1)
    g1 = 1.0 / (1.0 + e2)
    g2 = e2 / (1.0 + e2)
    return i1, g1, i2, g2


def _moe_ffn_kernel(x_ref, wr_ref, wg_ref, wu_ref, wd_ref, g_ref, b_ref, p_ref, wp_ref, wgate_ref, o_ref,
                    acc_ref, route_ref, *, alpha, f_exp):
    pr = pl.program_id(1)
    x = x_ref[...]
    tm = x.shape[0]

    @pl.when(pr == 0)
    def _():
        lane = lax.broadcasted_iota(jnp.int32, (tm, V7X_LANES), 1)
        logits = jnp.where(lane < N_EXPERTS, _hilo_dot(x, wr_ref[...]), -jnp.inf)
        i1, g1, i2, g2 = _top2_gates(logits)
        route_ref[...] = jnp.where(lane == i1, g1, 0.0) + jnp.where(lane == i2, g2, 0.0)
        acc_ref[...] = jnp.zeros_like(acc_ref)

    xb = x.astype(BF16)
    act = _silu(_dot(xb, wg_ref[...])) * _dot(xb, wu_ref[...])
    lane = lax.broadcasted_iota(jnp.int32, (tm, V7X_LANES), 1)
    gates = route_ref[...]
    gate_a = jnp.sum(jnp.where(lane == 2 * pr, gates, 0.0), axis=-1, keepdims=True)
    gate_b = jnp.sum(jnp.where(lane == 2 * pr + 1, gates, 0.0), axis=-1, keepdims=True)
    lo = (f_exp // V7X_LANES) * V7X_LANES
    hi = lo + V7X_LANES
    mixed = jnp.where(lane < f_exp - lo, gate_a, gate_b)
    act = jnp.concatenate([act[:, :lo] * gate_a, act[:, lo:hi] * mixed, act[:, hi:] * gate_b], axis=1)
    acc_ref[...] += _dot(act.astype(BF16), wd_ref[...])

    @pl.when(pr == pl.num_programs(1) - 1)
    def _():
        o_ref[...] = _ln_ple(x, acc_ref[...], g_ref[...], b_ref[...], p_ref[...], wp_ref[...], wgate_ref[...],
                             alpha)


def _moe_ffn(x, w_router2, w_g, w_u, w_d, g, b, p, w_proj, w_gate, alpha):
    t, d = x.shape
    f_exp = w_d.shape[0] // N_EXPERTS
    pd = p.shape[1]
    tm = ROW_BLOCK
    const = lambda i, pr: (0, 0)
    return pl.pallas_call(
        functools.partial(_moe_ffn_kernel, alpha=alpha, f_exp=f_exp),
        out_shape=jax.ShapeDtypeStruct((t, d), F32),
        grid=(t // tm, N_EXPERTS // 2),
        in_specs=[pl.BlockSpec((tm, d), lambda i, pr: (i, 0)),
                  _resident((d, V7X_LANES), const),
                  pl.BlockSpec((d, 2 * f_exp), lambda i, pr: (0, pr)),
                  pl.BlockSpec((d, 2 * f_exp), lambda i, pr: (0, pr)),
                  pl.BlockSpec((2 * f_exp, d), lambda i, pr: (pr, 0)),
                  _resident((1, d), const),
                  _resident((1, d), const),
                  pl.BlockSpec((tm, pd), lambda i, pr: (i, 0)),
                  _resident((pd, d), const),
                  _resident((d, d), const)],
        out_specs=pl.BlockSpec((tm, d), lambda i, pr: (i, 0)),
        scratch_shapes=[pltpu.VMEM((tm, d), F32), pltpu.VMEM((tm, V7X_LANES), F32)],
        compiler_params=_params("parallel", "arbitrary"),
        name="moe_ffn",
    )(x, w_router2, w_g, w_u, w_d, g, b, p, w_proj, w_gate)


def kernel(x, p, a_w_qkv, a_w_o, kvf_w, kvf_b_f, b_w_q, b_w_o, ffn_w_gu, ffn_w_down, moe_w_router, moe_w_gu,
           moe_w_down, ple_w_proj, ple_w_gate, ln_g, ln_b):
    bsz, seq, d = x.shape
    depth = p.shape[0]
    n_a = a_w_qkv.shape[0]
    t = bsz * seq
    dh = d // N_HEADS
    alpha = (2.0 * depth) ** 0.25
    scale = 1.0 / math.sqrt(dh)
    f_exp = moe_w_down.shape[2]

    xf = x.reshape(t, d)
    pf = p.reshape(depth, t, p.shape[-1])

    c_sh = kv_sh = None
    for i in range(depth):
        g1, b1 = ln_g[i, 0][None, :], ln_b[i, 0][None, :]
        g2, b2 = ln_g[i, 1][None, :], ln_b[i, 1][None, :]
        if i < n_a:
            w_qkv = jnp.concatenate([a_w_qkv[i][:, :d] * scale, a_w_qkv[i][:, d:]], axis=1).astype(BF16)
            qkv = _proj(xf, w_qkv).reshape(bsz, seq, 3 * d)
            mix = _sb_attention(qkv, d).reshape(t, d)
            w_o = a_w_o[i].astype(BF16)
        else:
            if i == n_a:
                kv, f_logit = _proj_kvf(xf, kvf_w[:, :2 * d].astype(BF16), _hilo_weight(kvf_w[:, 2 * d:]))
                kv_sh = kv.reshape(bsz, seq, 2 * d)
                c_sh = _log_forget_cumsum(f_logit, kvf_b_f, bsz, seq)
                c_sh = c_sh.reshape(bsz, N_HEADS, seq // ATTN_BLOCK, ATTN_BLOCK)
            j = i - n_a
            q = _proj(xf, (b_w_q[j] * scale).astype(BF16)).reshape(bsz, seq, d)
            mix = _fox_attention(q, kv_sh, c_sh, d).reshape(t, d)
            w_o = b_w_o[j].astype(BF16)
        xf = _out_ln(xf, mix, w_o, g1, b1, alpha)

        w_proj = ple_w_proj[i].astype(BF16)
        w_gate = ple_w_gate[i].astype(BF16)
        if i % 2 == 0:
            xf = _dense_ffn(xf, ffn_w_gu[i // 2].astype(BF16), ffn_w_down[i // 2].astype(BF16), g2, b2,
                            pf[i], w_proj, w_gate, alpha)
        else:
            m = i // 2
            w_g = moe_w_gu[m][:, :, :f_exp].transpose(1, 0, 2).reshape(d, N_EXPERTS * f_exp).astype(BF16)
            w_u = moe_w_gu[m][:, :, f_exp:].transpose(1, 0, 2).reshape(d, N_EXPERTS * f_exp).astype(BF16)
            w_d = moe_w_down[m].reshape(N_EXPERTS * f_exp, d).astype(BF16)
            xf = _moe_ffn(xf, _hilo_weight(moe_w_router[m]), w_g, w_u, w_d, g2, b2, pf[i], w_proj, w_gate,
                          alpha)
    return xf.reshape(bsz, seq, d)
```

```python
import functools
import math

import jax
import jax.numpy as jnp
from jax import lax
from jax.experimental import pallas as pl
from jax.experimental.pallas import tpu as pltpu

F32 = jnp.float32
BF16 = jnp.bfloat16

N_HEADS = 4
N_EXPERTS = 8
LN_EPS = 1e-5

V7X_LANES = 128
V7X_VMEM_LIMIT = 56 * 1024 * 1024

LOG2E = 1.4426950408889634
UNDERFLOW_EXP2 = 160.0

ROW_BLOCK = 512
ATTN_BLOCK = 256
ATTN_Q_TILE = 1024


def _params(*sem):
    return pltpu.CompilerParams(dimension_semantics=sem, vmem_limit_bytes=V7X_VMEM_LIMIT)


def _resident(shape, index_map):
    return pl.BlockSpec(shape, index_map, pipeline_mode=pl.Buffered(1))


def _dot(a, b):
    return jnp.dot(a, b, preferred_element_type=F32)


def _dot_nt(a, b):
    return lax.dot_general(a, b, (((1,), (1,)), ((), ())), preferred_element_type=F32)


def _split2(x):
    hi = x.astype(BF16)
    lo = (x - hi.astype(F32)).astype(BF16)
    return hi, lo


def _split3(x):
    a = x.astype(BF16)
    r = x - a.astype(F32)
    b = r.astype(BF16)
    c = (r - b.astype(F32)).astype(BF16)
    return a, b, c


def _hilo_weight(w):
    k, n = w.shape
    hi = w.astype(BF16)
    lo = (w - hi.astype(F32)).astype(BF16)
    out = jnp.zeros((k, V7X_LANES), BF16)
    out = out.at[:, :n].set(hi)
    out = out.at[:, 64:64 + n].set(lo)
    return out


def _hilo_dot(x, w2):
    xh, xl = _split2(x)
    a = _dot(xh, w2)
    b = _dot(xl, w2)
    return a + pltpu.roll(a, 64, 1) + b


def _layer_norm(y, g, b):
    mu = jnp.mean(y, axis=-1, keepdims=True)
    d = y - mu
    var = jnp.mean(d * d, axis=-1, keepdims=True)
    return d * lax.rsqrt(var + LN_EPS) * g + b


def _proj_kernel(x_ref, w_ref, o_ref):
    o_ref[...] = _dot(x_ref[...].astype(BF16), w_ref[...]).astype(o_ref.dtype)


def _proj(x, w):
    t, k = x.shape
    n = w.shape[1]
    return pl.pallas_call(
        _proj_kernel,
        out_shape=jax.ShapeDtypeStruct((t, n), BF16),
        grid=(t // ROW_BLOCK,),
        in_specs=[pl.BlockSpec((ROW_BLOCK, k), lambda i: (i, 0)),
                  _resident((k, n), lambda i: (0, 0))],
        out_specs=pl.BlockSpec((ROW_BLOCK, n), lambda i: (i, 0)),
        compiler_params=_params("parallel"),
        name="proj",
    )(x, w)


def _proj_kvf_kernel(x_ref, w_ref, wf_ref, o_ref, f_ref):
    x = x_ref[...]
    o_ref[...] = _dot(x.astype(BF16), w_ref[...]).astype(o_ref.dtype)
    f_ref[...] = _hilo_dot(x, wf_ref[...])


def _proj_kvf(x, w, wf2):
    t, k = x.shape
    n = w.shape[1]
    return pl.pallas_call(
        _proj_kvf_kernel,
        out_shape=(jax.ShapeDtypeStruct((t, n), BF16),
                   jax.ShapeDtypeStruct((t, V7X_LANES), F32)),
        grid=(t // ROW_BLOCK,),
        in_specs=[pl.BlockSpec((ROW_BLOCK, k), lambda i: (i, 0)),
                  _resident((k, n), lambda i: (0, 0)),
                  _resident((k, V7X_LANES), lambda i: (0, 0))],
        out_specs=(pl.BlockSpec((ROW_BLOCK, n), lambda i: (i, 0)),
                   pl.BlockSpec((ROW_BLOCK, V7X_LANES), lambda i: (i, 0))),
        compiler_params=_params("parallel"),
        name="proj_kvf",
    )(x, w, wf2)


def _cumsum_kernel(v_ref, bias_ref, c_ref, *, rows_per_seq):
    z = v_ref[...] + bias_ref[...]
    a = (jnp.minimum(z, 0.0) - jnp.log(1.0 + jnp.exp(-jnp.abs(z)))) * LOG2E
    r = a.shape[0]
    j = lax.broadcasted_iota(jnp.int32, (V7X_LANES, V7X_LANES), 0)
    s = lax.broadcasted_iota(jnp.int32, (V7X_LANES, V7X_LANES), 1)
    upper = jnp.where(j <= s, 1.0, 0.0).astype(BF16)
    a1, a2, a3 = _split3(a)
    within = _dot(a1, upper) + _dot(a2, upper) + _dot(a3, upper)
    ri = lax.broadcasted_iota(jnp.int32, (r, r), 0)
    rj = lax.broadcasted_iota(jnp.int32, (r, r), 1)
    same = (ri // rows_per_seq) == (rj // rows_per_seq)
    before = jnp.where(same & (rj < ri), 1.0, 0.0).astype(BF16)
    w1, w2, w3 = _split3(within)
    offs = _dot(before, w1) + _dot(before, w2) + _dot(before, w3)
    c_ref[...] = within + offs[:, V7X_LANES - 1:V7X_LANES]


def _log_forget_cumsum(f_logit, bias_f, bsz, seq):
    rows_per_seq = seq // V7X_LANES
    fl = f_logit.reshape(bsz, seq, V7X_LANES)[:, :, :N_HEADS].transpose(0, 2, 1)
    fl = fl.reshape(bsz * N_HEADS * rows_per_seq, V7X_LANES)
    bias = jnp.broadcast_to(bias_f.astype(F32)[None, :, None, None],
                            (bsz, N_HEADS, rows_per_seq, 1)).reshape(-1, 1)
    c = pl.pallas_call(
        functools.partial(_cumsum_kernel, rows_per_seq=rows_per_seq),
        out_shape=jax.ShapeDtypeStruct(fl.shape, F32),
        compiler_params=_params(),
        name="forget_cumsum",
    )(fl, bias)
    return c.reshape(bsz, N_HEADS, seq)


def _softplus2(z):
    return jnp.maximum(z, 0.0) + jnp.log(1.0 + jnp.exp2(-jnp.abs(z))) * LOG2E


def _sb_attn_kernel(q_ref, k_ref, v_ref, o_ref, acc_ref, carry_ref, *, blk):
    qi = pl.program_id(2)
    groups = q_ref.shape[0] // blk
    first_diag = qi * groups
    row = lax.broadcasted_iota(jnp.int32, (blk, blk), 0)
    col = lax.broadcasted_iota(jnp.int32, (blk, blk), 1)
    causal = col < row
    suffix = jnp.where(row >= col, 1.0, 0.0).astype(BF16)

    def group_tile(g, first_blk, masks):
        n = len(masks)
        rows = pl.ds(g * blk, blk)
        start = pl.multiple_of(first_blk * blk, blk)
        k = k_ref[pl.ds(start, n * blk), :]
        v = v_ref[pl.ds(start, n * blk), :]
        z = _dot_nt(q_ref[rows, :], k)
        later = carry_ref[rows, :]
        ps = [None] * n
        for i in reversed(range(n)):
            zi = z[:, i * blk:(i + 1) * blk]
            sp = _softplus2(zi)
            if masks[i] is not None:
                sp = jnp.where(masks[i], sp, 0.0)
            r = _dot(sp.astype(BF16), suffix) + later
            p = jnp.exp2(zi - r)
            if masks[i] is not None:
                p = jnp.where(masks[i], p, 0.0)
            ps[i] = p.astype(BF16)
            later = r[:, 0:1]
        acc_ref[rows, :] += _dot(jnp.concatenate(ps, axis=1) if n > 1 else ps[0], v)
        carry_ref[rows, :] = later

    acc_ref[...] = jnp.zeros_like(acc_ref)
    carry_ref[...] = jnp.zeros_like(carry_ref)

    @pl.when(qi == 0)
    def _():
        group_tile(0, 0, [causal])
        for g in range(1, groups):
            group_tile(g, g - 1, [None, causal])

    @pl.when(qi > 0)
    def _():
        for g in range(groups):
            group_tile(g, first_diag + g - 1, [None, causal])

    def live():
        return jnp.min(carry_ref[...]) < UNDERFLOW_EXP2

    @pl.when(live() & (qi > 0))
    def _():
        for g in range(1, groups):
            group_tile(g, first_diag - 1, [None] * g)

    if groups > 2:
        @pl.when(live() & (qi == 0))
        def _():
            for g in range(2, groups):
                group_tile(g, 0, [None] * (g - 1))

    def body(state):
        n, _ = state
        for g in range(groups):
            group_tile(g, n - 2, [None, None])
        return n - 2, live()

    n, alive = lax.while_loop(lambda st: (st[0] >= 2) & st[1], body, (jnp.maximum(first_diag - 1, 0), live()))

    @pl.when((n == 1) & alive)
    def _():
        for g in range(groups):
            group_tile(g, 0, [None])

    o_ref[...] = acc_ref[...].astype(o_ref.dtype)


def _sb_attention(qkv, d_model):
    b, s, _ = qkv.shape
    dh = d_model // N_HEADS
    tq = ATTN_Q_TILE
    return pl.pallas_call(
        functools.partial(_sb_attn_kernel, blk=ATTN_BLOCK),
        out_shape=jax.ShapeDtypeStruct((b, s, d_model), BF16),
        grid=(b, N_HEADS, s // tq),
        in_specs=[pl.BlockSpec((None, tq, dh), lambda bi, h, qi: (bi, qi, h)),
                  _resident((None, s, dh), lambda bi, h, qi: (bi, 0, N_HEADS + h)),
                  _resident((None, s, dh), lambda bi, h, qi: (bi, 0, 2 * N_HEADS + h))],
        out_specs=pl.BlockSpec((None, tq, dh), lambda bi, h, qi: (bi, qi, h)),
        scratch_shapes=[pltpu.VMEM((tq, dh), F32), pltpu.VMEM((tq, 1), F32)],
        compiler_params=_params("parallel", "parallel", "arbitrary"),
        name="sb_attention",
    )(qkv, qkv, qkv)


def _row_norm_max(x):
    xf = x.astype(F32)
    return jnp.max(jnp.sqrt(jnp.sum(xf * xf, axis=-1, keepdims=True)))


def _fox_attn_kernel(cs_ref, ce_ref, q_ref, k_ref, v_ref, c_ref, o_ref, acc_ref, m_ref, l_ref, kn_ref, *, blk):
    bi, h, qi = pl.program_id(0), pl.program_id(1), pl.program_id(2)
    q = q_ref[...]
    groups = q.shape[0] // blk
    first_diag = qi * groups
    row = lax.broadcasted_iota(jnp.int32, (blk, blk), 0)
    col = lax.broadcasted_iota(jnp.int32, (blk, blk), 1)
    causal = col <= row

    @pl.when(qi == 0)
    def _():
        def chunk(i, best):
            return jnp.maximum(best, _row_norm_max(k_ref[pl.ds(pl.multiple_of(i * blk, blk), blk), :]))
        kn_ref[0] = lax.fori_loop(0, k_ref.shape[0] // blk, chunk, jnp.float32(0.0))

    qk_bound = 1.01 * _row_norm_max(q) * kn_ref[0]
    c_first = cs_ref[bi, h, first_diag]

    def group_tile(g, first_blk, masks):
        n = len(masks)
        rows = pl.ds(g * blk, blk)
        start = pl.multiple_of(first_blk * blk, blk)
        k = k_ref[pl.ds(start, n * blk), :]
        v = v_ref[pl.ds(start, n * blk), :]
        z = _dot_nt(q_ref[rows, :], k)
        zs = []
        for i in range(n):
            zi = z[:, i * blk:(i + 1) * blk] + (c_first - c_ref[pl.ds(first_blk + i, 1), :])
            if masks[i] is not None:
                zi = jnp.where(masks[i], zi, -jnp.inf)
            zs.append(zi)
        z = jnp.concatenate(zs, axis=1) if n > 1 else zs[0]
        m_old = m_ref[rows, :]
        m_new = jnp.maximum(m_old, jnp.max(z, axis=-1, keepdims=True))
        p = jnp.exp2(z - m_new)
        alpha = jnp.exp2(m_old - m_new)
        l_ref[rows, :] = alpha * l_ref[rows, :] + jnp.sum(p, axis=-1, keepdims=True)
        acc_ref[rows, :] = alpha * acc_ref[rows, :] + _dot(p.astype(BF16), v)
        m_ref[rows, :] = m_new

    acc_ref[...] = jnp.zeros_like(acc_ref)
    l_ref[...] = jnp.zeros_like(l_ref)
    m_ref[...] = jnp.full_like(m_ref, -jnp.inf)

    @pl.when(qi == 0)
    def _():
        group_tile(0, 0, [causal])
        for g in range(1, groups):
            group_tile(g, g - 1, [None, causal])

    @pl.when(qi > 0)
    def _():
        for g in range(groups):
            group_tile(g, first_diag + g - 1, [None, causal])

    def live(n):
        bound = qk_bound + (c_first - ce_ref[bi, h, jnp.maximum(n - 1, 0)])
        return bound > jnp.min(m_ref[...]) - UNDERFLOW_EXP2

    @pl.when(live(first_diag + groups - 2) & (qi > 0))
    def _():
        for g in range(1, groups):
            group_tile(g, first_diag - 1, [None] * g)

    if groups > 2:
        @pl.when(live(groups - 2) & (qi == 0))
        def _():
            for g in range(2, groups):
                group_tile(g, 0, [None] * (g - 1))

    def body(state):
        n, _ = state
        for g in range(groups):
            group_tile(g, n - 2, [None, None])
        return n - 2, live(n - 2)

    n0 = jnp.maximum(first_diag - 1, 0)
    n, alive = lax.while_loop(lambda st: (st[0] >= 2) & st[1], body, (n0, live(n0)))

    @pl.when((n == 1) & alive)
    def _():
        for g in range(groups):
            group_tile(g, 0, [None])

    o_ref[...] = (acc_ref[...] / l_ref[...]).astype(o_ref.dtype)


def _fox_attention(q, kv, c, d_model):
    b, s, _ = q.shape
    dh = d_model // N_HEADS
    blk = ATTN_BLOCK
    tq = ATTN_Q_TILE
    return pl.pallas_call(
        functools.partial(_fox_attn_kernel, blk=blk),
        out_shape=jax.ShapeDtypeStruct((b, s, d_model), BF16),
        grid=(b, N_HEADS, s // tq),
        in_specs=[pl.BlockSpec(memory_space=pltpu.SMEM),
                  pl.BlockSpec(memory_space=pltpu.SMEM),
                  pl.BlockSpec((None, tq, dh), lambda bi, h, qi: (bi, qi, h)),
                  _resident((None, s, dh), lambda bi, h, qi: (bi, 0, h)),
                  _resident((None, s, dh), lambda bi, h, qi: (bi, 0, N_HEADS + h)),
                  _resident((None, None, s // blk, blk), lambda bi, h, qi: (bi, h, 0, 0))],
        out_specs=pl.BlockSpec((None, tq, dh), lambda bi, h, qi: (bi, qi, h)),
        scratch_shapes=[pltpu.VMEM((tq, dh), F32), pltpu.VMEM((tq, 1), F32), pltpu.VMEM((tq, 1), F32),
                        pltpu.SMEM((1,), F32)],
        compiler_params=_params("parallel", "parallel", "arbitrary"),
        name="fox_attention",
    )(c[..., 0], c[..., -1], q, kv, kv, c)


def _silu(g):
    return g / (1.0 + jnp.exp(-g))


def _ln_ple(x, ff, g, b, p, w_proj, w_gate, alpha):
    y = _layer_norm(alpha * x + ff, g, b)
    gate = 1.0 / (1.0 + jnp.exp(-_dot(y.astype(BF16), w_gate)))
    return y + _dot(p.astype(BF16), w_proj) * gate


def _mix_ln(x_ref, a_ref, wo_ref, g_ref, b_ref, alpha):
    return _layer_norm(alpha * x_ref[...] + _dot(a_ref[...], wo_ref[...]), g_ref[...], b_ref[...])


def _dense_ffn_kernel(x_ref, a_ref, wo_ref, g1_ref, b1_ref, wgu_ref, wd_ref, g_ref, b_ref, p_ref, wp_ref, wg_ref,
                      o_ref, *, alpha, d_ff):
    x = _mix_ln(x_ref, a_ref, wo_ref, g1_ref, b1_ref, alpha)
    h = _dot(x.astype(BF16), wgu_ref[...])
    act = (_silu(h[:, :d_ff]) * h[:, d_ff:]).astype(BF16)
    ff = _dot(act, wd_ref[...])
    o_ref[...] = _ln_ple(x, ff, g_ref[...], b_ref[...], p_ref[...], wp_ref[...], wg_ref[...], alpha)


DENSE_ROW_BLOCK = 256


def _dense_ffn(x, a, w_o, g1, b1, w_gu, w_down, g, b, p, w_proj, w_gate, alpha):
    t, d = x.shape
    d_ff = w_down.shape[0]
    pd = p.shape[1]
    tm = DENSE_ROW_BLOCK
    const = lambda i: (0, 0)
    return pl.pallas_call(
        functools.partial(_dense_ffn_kernel, alpha=alpha, d_ff=d_ff),
        out_shape=jax.ShapeDtypeStruct((t, d), F32),
        grid=(t // tm,),
        in_specs=[pl.BlockSpec((tm, d), lambda i: (i, 0)),
                  pl.BlockSpec((tm, d), lambda i: (i, 0)),
                  _resident((d, d), const),
                  _resident((1, d), const),
                  _resident((1, d), const),
                  _resident((d, 2 * d_ff), const),
                  _resident((d_ff, d), const),
                  _resident((1, d), const),
                  _resident((1, d), const),
                  pl.BlockSpec((tm, pd), lambda i: (i, 0)),
                  _resident((pd, d), const),
                  _resident((d, d), const)],
        out_specs=pl.BlockSpec((tm, d), lambda i: (i, 0)),
        compiler_params=_params("parallel"),
        name="dense_ffn",
    )(x, a, w_o, g1, b1, w_gu, w_down, g, b, p, w_proj, w_gate)


def _top2_gates(logits):
    lane = lax.broadcasted_iota(jnp.int32, logits.shape, 1)
    v1 = jnp.max(logits, axis=-1, keepdims=True)
    i1 = jnp.min(jnp.where(logits == v1, lane, V7X_LANES), axis=-1, keepdims=True)
    rest = jnp.where(lane == i1, -jnp.inf, logits)
    v2 = jnp.max(rest, axis=-1, keepdims=True)
    i2 = jnp.min(jnp.where(rest == v2, lane, V7X_LANES), axis=-1, keepdims=True)
    e2 = jnp.exp(v2 - v1)
    g1 = 1.0 / (1.0 + e2)
    g2 = e2 / (1.0 + e2)
    return i1, g1, i2, g2


def _moe_ffn_kernel(x_ref, a_ref, wo_ref, g1_ref, b1_ref, wr_ref, wg_ref, wu_ref, wd_ref, g_ref, b_ref, p_ref,
                    wp_ref, wgate_ref, o_ref, acc_ref, route_ref, x1_ref, xb_ref, *, alpha, f_exp):
    pr = pl.program_id(1)
    tm = x_ref.shape[0]

    @pl.when(pr == 0)
    def _():
        x = _mix_ln(x_ref, a_ref, wo_ref, g1_ref, b1_ref, alpha)
        x1_ref[...] = x
        xb_ref[...] = x.astype(BF16)
        lane = lax.broadcasted_iota(jnp.int32, (tm, V7X_LANES), 1)
        logits = jnp.where(lane < N_EXPERTS, _hilo_dot(x, wr_ref[...]), -jnp.inf)
        i1, g1, i2, g2 = _top2_gates(logits)
        route_ref[...] = jnp.where(lane == i1, g1, 0.0) + jnp.where(lane == i2, g2, 0.0)
        acc_ref[...] = jnp.zeros_like(acc_ref)

    xb = xb_ref[...]
    act = _silu(_dot(xb, wg_ref[...])) * _dot(xb, wu_ref[...])
    lane = lax.broadcasted_iota(jnp.int32, (tm, V7X_LANES), 1)
    gates = route_ref[...]
    gate_a = jnp.sum(jnp.where(lane == 2 * pr, gates, 0.0), axis=-1, keepdims=True)
    gate_b = jnp.sum(jnp.where(lane == 2 * pr + 1, gates, 0.0), axis=-1, keepdims=True)
    lo = (f_exp // V7X_LANES) * V7X_LANES
    hi = lo + V7X_LANES
    mixed = jnp.where(lane < f_exp - lo, gate_a, gate_b)
    act = jnp.concatenate([act[:, :lo] * gate_a, act[:, lo:hi] * mixed, act[:, hi:] * gate_b], axis=1)
    acc_ref[...] += _dot(act.astype(BF16), wd_ref[...])

    @pl.when(pr == pl.num_programs(1) - 1)
    def _():
        o_ref[...] = _ln_ple(x1_ref[...], acc_ref[...], g_ref[...], b_ref[...], p_ref[...], wp_ref[...],
                             wgate_ref[...], alpha)


def _moe_ffn(x, a, w_o, g1, b1, w_router2, w_g, w_u, w_d, g, b, p, w_proj, w_gate, alpha):
    t, d = x.shape
    f_exp = w_d.shape[0] // N_EXPERTS
    pd = p.shape[1]
    tm = ROW_BLOCK
    const = lambda i, pr: (0, 0)
    return pl.pallas_call(
        functools.partial(_moe_ffn_kernel, alpha=alpha, f_exp=f_exp),
        out_shape=jax.ShapeDtypeStruct((t, d), F32),
        grid=(t // tm, N_EXPERTS // 2),
        in_specs=[pl.BlockSpec((tm, d), lambda i, pr: (i, 0)),
                  pl.BlockSpec((tm, d), lambda i, pr: (i, 0)),
                  _resident((d, d), const),
                  _resident((1, d), const),
                  _resident((1, d), const),
                  _resident((d, V7X_LANES), const),
                  pl.BlockSpec((d, 2 * f_exp), lambda i, pr: (0, pr)),
                  pl.BlockSpec((d, 2 * f_exp), lambda i, pr: (0, pr)),
                  pl.BlockSpec((2 * f_exp, d), lambda i, pr: (pr, 0)),
                  _resident((1, d), const),
                  _resident((1, d), const),
                  pl.BlockSpec((tm, pd), lambda i, pr: (i, 0)),
                  _resident((pd, d), const),
                  _resident((d, d), const)],
        out_specs=pl.BlockSpec((tm, d), lambda i, pr: (i, 0)),
        scratch_shapes=[pltpu.VMEM((tm, d), F32), pltpu.VMEM((tm, V7X_LANES), F32),
                        pltpu.VMEM((tm, d), F32), pltpu.VMEM((tm, d), BF16)],
        compiler_params=_params("parallel", "arbitrary"),
        name="moe_ffn",
    )(x, a, w_o, g1, b1, w_router2, w_g, w_u, w_d, g, b, p, w_proj, w_gate)


def kernel(x, p, a_w_qkv, a_w_o, kvf_w, kvf_b_f, b_w_q, b_w_o, ffn_w_gu, ffn_w_down, moe_w_router, moe_w_gu,
           moe_w_down, ple_w_proj, ple_w_gate, ln_g, ln_b):
    bsz, seq, d = x.shape
    depth = p.shape[0]
    n_a = a_w_qkv.shape[0]
    t = bsz * seq
    dh = d // N_HEADS
    alpha = (2.0 * depth) ** 0.25
    scale = LOG2E / math.sqrt(dh)
    f_exp = moe_w_down.shape[2]

    xf = x.reshape(t, d)
    pf = p.reshape(depth, t, p.shape[-1])

    c_sh = kv_sh = None
    for i in range(depth):
        g1, b1 = ln_g[i, 0][None, :], ln_b[i, 0][None, :]
        g2, b2 = ln_g[i, 1][None, :], ln_b[i, 1][None, :]
        if i < n_a:
            w_qkv = jnp.concatenate([a_w_qkv[i][:, :d] * scale, a_w_qkv[i][:, d:]], axis=1).astype(BF16)
            qkv = _proj(xf, w_qkv).reshape(bsz, seq, 3 * d)
            mix = _sb_attention(qkv, d).reshape(t, d)
            w_o = a_w_o[i].astype(BF16)
        else:
            if i == n_a:
                kv, f_logit = _proj_kvf(xf, kvf_w[:, :2 * d].astype(BF16), _hilo_weight(kvf_w[:, 2 * d:]))
                kv_sh = kv.reshape(bsz, seq, 2 * d)
                c_sh = _log_forget_cumsum(f_logit, kvf_b_f, bsz, seq)
                c_sh = c_sh.reshape(bsz, N_HEADS, seq // ATTN_BLOCK, ATTN_BLOCK)
            j = i - n_a
            q = _proj(xf, (b_w_q[j] * scale).astype(BF16)).reshape(bsz, seq, d)
            mix = _fox_attention(q, kv_sh, c_sh, d).reshape(t, d)
            w_o = b_w_o[j].astype(BF16)

        w_proj = ple_w_proj[i].astype(BF16)
        w_gate = ple_w_gate[i].astype(BF16)
        if i % 2 == 0:
            xf = _dense_ffn(xf, mix, w_o, g1, b1, ffn_w_gu[i // 2].astype(BF16), ffn_w_down[i // 2].astype(BF16),
                            g2, b2, pf[i], w_proj, w_gate, alpha)
        else:
            m = i // 2
            w_g = moe_w_gu[m][:, :, :f_exp].transpose(1, 0, 2).reshape(d, N_EXPERTS * f_exp).astype(BF16)
            w_u = moe_w_gu[m][:, :, f_exp:].transpose(1, 0, 2).reshape(d, N_EXPERTS * f_exp).astype(BF16)
            w_d = moe_w_down[m].reshape(N_EXPERTS * f_exp, d).astype(BF16)
            xf = _moe_ffn(xf, mix, w_o, g1, b1, _hilo_weight(moe_w_router[m]), w_g, w_u, w_d, g2, b2, pf[i],
                          w_proj, w_gate, alpha)
    return xf.reshape(bsz, seq, d)
```

```python
import functools
import math

import jax
import jax.numpy as jnp
from jax import lax
from jax.experimental import pallas as pl
from jax.experimental.pallas import tpu as pltpu

F32 = jnp.float32
BF16 = jnp.bfloat16

N_HEADS = 4
N_EXPERTS = 8
LN_EPS = 1e-5

V7X_LANES = 128
V7X_VMEM_LIMIT = 56 * 1024 * 1024

LOG2E = 1.4426950408889634
UNDERFLOW_EXP2 = 160.0
OVERFLOW_SAFE_EXP2 = 64.0

ROW_BLOCK = 512
ATTN_BLOCK = 256
ATTN_Q_TILE = 1024


def _params(*sem):
    return pltpu.CompilerParams(dimension_semantics=sem, vmem_limit_bytes=V7X_VMEM_LIMIT)


def _resident(shape, index_map):
    return pl.BlockSpec(shape, index_map, pipeline_mode=pl.Buffered(1))


def _dot(a, b):
    return jnp.dot(a, b, preferred_element_type=F32)


def _dot_nt(a, b):
    return lax.dot_general(a, b, (((1,), (1,)), ((), ())), preferred_element_type=F32)


def _split2(x):
    hi = x.astype(BF16)
    lo = (x - hi.astype(F32)).astype(BF16)
    return hi, lo


def _split3(x):
    a = x.astype(BF16)
    r = x - a.astype(F32)
    b = r.astype(BF16)
    c = (r - b.astype(F32)).astype(BF16)
    return a, b, c


def _hilo_weight(w):
    k, n = w.shape
    hi = w.astype(BF16)
    lo = (w - hi.astype(F32)).astype(BF16)
    out = jnp.zeros((k, V7X_LANES), BF16)
    out = out.at[:, :n].set(hi)
    out = out.at[:, 64:64 + n].set(lo)
    return out


def _hilo_dot(x, w2):
    xh, xl = _split2(x)
    a = _dot(xh, w2)
    b = _dot(xl, w2)
    return a + pltpu.roll(a, 64, 1) + b


def _layer_norm(y, g, b):
    mu = jnp.mean(y, axis=-1, keepdims=True)
    d = y - mu
    var = jnp.mean(d * d, axis=-1, keepdims=True)
    return d * lax.rsqrt(var + LN_EPS) * g + b


def _proj_kernel(x_ref, w_ref, o_ref):
    o_ref[...] = _dot(x_ref[...].astype(BF16), w_ref[...]).astype(o_ref.dtype)


def _proj(x, w):
    t, k = x.shape
    n = w.shape[1]
    return pl.pallas_call(
        _proj_kernel,
        out_shape=jax.ShapeDtypeStruct((t, n), BF16),
        grid=(t // ROW_BLOCK,),
        in_specs=[pl.BlockSpec((ROW_BLOCK, k), lambda i: (i, 0)),
                  _resident((k, n), lambda i: (0, 0))],
        out_specs=pl.BlockSpec((ROW_BLOCK, n), lambda i: (i, 0)),
        compiler_params=_params("parallel"),
        name="proj",
    )(x, w)


def _proj_kvf_kernel(x_ref, w_ref, wf_ref, o_ref, f_ref):
    x = x_ref[...]
    o_ref[...] = _dot(x.astype(BF16), w_ref[...]).astype(o_ref.dtype)
    f_ref[...] = _hilo_dot(x, wf_ref[...])


def _proj_kvf(x, w, wf2):
    t, k = x.shape
    n = w.shape[1]
    return pl.pallas_call(
        _proj_kvf_kernel,
        out_shape=(jax.ShapeDtypeStruct((t, n), BF16),
                   jax.ShapeDtypeStruct((t, V7X_LANES), F32)),
        grid=(t // ROW_BLOCK,),
        in_specs=[pl.BlockSpec((ROW_BLOCK, k), lambda i: (i, 0)),
                  _resident((k, n), lambda i: (0, 0)),
                  _resident((k, V7X_LANES), lambda i: (0, 0))],
        out_specs=(pl.BlockSpec((ROW_BLOCK, n), lambda i: (i, 0)),
                   pl.BlockSpec((ROW_BLOCK, V7X_LANES), lambda i: (i, 0))),
        compiler_params=_params("parallel"),
        name="proj_kvf",
    )(x, w, wf2)


def _cumsum_kernel(v_ref, bias_ref, c_ref, *, rows_per_seq):
    z = v_ref[...] + bias_ref[...]
    a = (jnp.minimum(z, 0.0) - jnp.log(1.0 + jnp.exp(-jnp.abs(z)))) * LOG2E
    r = a.shape[0]
    j = lax.broadcasted_iota(jnp.int32, (V7X_LANES, V7X_LANES), 0)
    s = lax.broadcasted_iota(jnp.int32, (V7X_LANES, V7X_LANES), 1)
    upper = jnp.where(j <= s, 1.0, 0.0).astype(BF16)
    a1, a2, a3 = _split3(a)
    within = _dot(a1, upper) + _dot(a2, upper) + _dot(a3, upper)
    ri = lax.broadcasted_iota(jnp.int32, (r, r), 0)
    rj = lax.broadcasted_iota(jnp.int32, (r, r), 1)
    same = (ri // rows_per_seq) == (rj // rows_per_seq)
    before = jnp.where(same & (rj < ri), 1.0, 0.0).astype(BF16)
    w1, w2, w3 = _split3(within)
    offs = _dot(before, w1) + _dot(before, w2) + _dot(before, w3)
    c_ref[...] = within + offs[:, V7X_LANES - 1:V7X_LANES]


def _log_forget_cumsum(f_logit, bias_f, bsz, seq):
    rows_per_seq = seq // V7X_LANES
    fl = f_logit.reshape(bsz, seq, V7X_LANES)[:, :, :N_HEADS].transpose(0, 2, 1)
    fl = fl.reshape(bsz * N_HEADS * rows_per_seq, V7X_LANES)
    bias = jnp.broadcast_to(bias_f.astype(F32)[None, :, None, None],
                            (bsz, N_HEADS, rows_per_seq, 1)).reshape(-1, 1)
    c = pl.pallas_call(
        functools.partial(_cumsum_kernel, rows_per_seq=rows_per_seq),
        out_shape=jax.ShapeDtypeStruct(fl.shape, F32),
        compiler_params=_params(),
        name="forget_cumsum",
    )(fl, bias)
    return c.reshape(bsz, N_HEADS, seq)


def _softplus2(z):
    return jnp.maximum(z, 0.0) + jnp.log(1.0 + jnp.exp2(-jnp.abs(z))) * LOG2E


def _sb_attn_kernel(q_ref, k_ref, v_ref, o_ref, acc_ref, carry_ref, *, blk):
    qi = pl.program_id(2)
    groups = q_ref.shape[0] // blk
    first_diag = qi * groups
    row = lax.broadcasted_iota(jnp.int32, (blk, blk), 0)
    col = lax.broadcasted_iota(jnp.int32, (blk, blk), 1)
    causal = col < row
    suffix = jnp.where(row >= col, 1.0, 0.0).astype(BF16)

    def group_tile(g, first_blk, masks):
        n = len(masks)
        rows = pl.ds(g * blk, blk)
        start = pl.multiple_of(first_blk * blk, blk)
        k = k_ref[pl.ds(start, n * blk), :]
        v = v_ref[pl.ds(start, n * blk), :]
        z = _dot_nt(q_ref[rows, :], k)
        later = carry_ref[rows, :]
        ps = [None] * n
        for i in reversed(range(n)):
            zi = z[:, i * blk:(i + 1) * blk]
            sp = _softplus2(zi)
            if masks[i] is not None:
                sp = jnp.where(masks[i], sp, 0.0)
            r = _dot(sp.astype(BF16), suffix) + later
            p = jnp.exp2(zi - r)
            if masks[i] is not None:
                p = jnp.where(masks[i], p, 0.0)
            ps[i] = p.astype(BF16)
            later = r[:, 0:1]
        acc_ref[rows, :] += _dot(jnp.concatenate(ps, axis=1) if n > 1 else ps[0], v)
        carry_ref[rows, :] = later

    acc_ref[...] = jnp.zeros_like(acc_ref)
    carry_ref[...] = jnp.zeros_like(carry_ref)

    @pl.when(qi == 0)
    def _():
        group_tile(0, 0, [causal])
        for g in range(1, groups):
            group_tile(g, g - 1, [None, causal])

    @pl.when(qi > 0)
    def _():
        for g in range(groups):
            group_tile(g, first_diag + g - 1, [None, causal])

    def live():
        return jnp.min(carry_ref[...]) < UNDERFLOW_EXP2

    @pl.when(live() & (qi > 0))
    def _():
        for g in range(1, groups):
            group_tile(g, first_diag - 1, [None] * g)

    if groups > 2:
        @pl.when(live() & (qi == 0))
        def _():
            for g in range(2, groups):
                group_tile(g, 0, [None] * (g - 1))

    def body(state):
        n, _ = state
        for g in range(groups):
            group_tile(g, n - 2, [None, None])
        return n - 2, live()

    n, alive = lax.while_loop(lambda st: (st[0] >= 2) & st[1], body, (jnp.maximum(first_diag - 1, 0), live()))

    @pl.when((n == 1) & alive)
    def _():
        for g in range(groups):
            group_tile(g, 0, [None])

    o_ref[...] = acc_ref[...].astype(o_ref.dtype)


def _sb_attention(qkv, d_model):
    b, s, _ = qkv.shape
    dh = d_model // N_HEADS
    tq = ATTN_Q_TILE
    return pl.pallas_call(
        functools.partial(_sb_attn_kernel, blk=ATTN_BLOCK),
        out_shape=jax.ShapeDtypeStruct((b, s, d_model), BF16),
        grid=(b, N_HEADS, s // tq),
        in_specs=[pl.BlockSpec((None, tq, dh), lambda bi, h, qi: (bi, qi, h)),
                  _resident((None, s, dh), lambda bi, h, qi: (bi, 0, N_HEADS + h)),
                  _resident((None, s, dh), lambda bi, h, qi: (bi, 0, 2 * N_HEADS + h))],
        out_specs=pl.BlockSpec((None, tq, dh), lambda bi, h, qi: (bi, qi, h)),
        scratch_shapes=[pltpu.VMEM((tq, dh), F32), pltpu.VMEM((tq, 1), F32)],
        compiler_params=_params("parallel", "parallel", "arbitrary"),
        name="sb_attention",
    )(qkv, qkv, qkv)


def _row_norm_max(x):
    xf = x.astype(F32)
    return jnp.max(jnp.sqrt(jnp.sum(xf * xf, axis=-1, keepdims=True)))


def _fox_attn_kernel(cs_ref, ce_ref, q_ref, k_ref, v_ref, c_ref, o_ref, acc_ref, m_ref, l_ref, kn_ref, *, blk):
    bi, h, qi = pl.program_id(0), pl.program_id(1), pl.program_id(2)
    q = q_ref[...]
    groups = q.shape[0] // blk
    first_diag = qi * groups
    row = lax.broadcasted_iota(jnp.int32, (blk, blk), 0)
    col = lax.broadcasted_iota(jnp.int32, (blk, blk), 1)
    causal = col <= row

    @pl.when(qi == 0)
    def _():
        def chunk(i, best):
            return jnp.maximum(best, _row_norm_max(k_ref[pl.ds(pl.multiple_of(i * blk, blk), blk), :]))
        kn_ref[0] = lax.fori_loop(0, k_ref.shape[0] // blk, chunk, jnp.float32(0.0))

    qk_bound = 1.01 * _row_norm_max(q) * kn_ref[0]
    c_first = [cs_ref[bi, h, first_diag + g] for g in range(groups)]

    def group_tile(g, first_blk, masks, new_max):
        n = len(masks)
        rows = pl.ds(g * blk, blk)
        start = pl.multiple_of(first_blk * blk, blk)
        k = k_ref[pl.ds(start, n * blk), :]
        v = v_ref[pl.ds(start, n * blk), :]
        z = _dot_nt(q_ref[rows, :], k)
        zs = []
        for i in range(n):
            zi = z[:, i * blk:(i + 1) * blk] + (c_first[g] - c_ref[pl.ds(first_blk + i, 1), :])
            if masks[i] is not None:
                zi = jnp.where(masks[i], zi, -jnp.inf)
            zs.append(zi)
        z = jnp.concatenate(zs, axis=1) if n > 1 else zs[0]
        m_old = m_ref[rows, :]
        if new_max:
            m_new = jnp.maximum(m_old, jnp.max(z, axis=-1, keepdims=True))
            m_ref[rows, :] = m_new
        else:
            m_new = m_old
        p = jnp.exp2(z - m_new)
        lanes = sum(p[:, j * V7X_LANES:(j + 1) * V7X_LANES] for j in range(p.shape[1] // V7X_LANES))
        pv = _dot(p.astype(BF16), v)
        if new_max:
            alpha = jnp.exp2(m_old - m_new)
            l_ref[rows, :] = alpha * l_ref[rows, :] + lanes
            acc_ref[rows, :] = alpha * acc_ref[rows, :] + pv
        else:
            l_ref[rows, :] += lanes
            acc_ref[rows, :] += pv

    acc_ref[...] = jnp.zeros_like(acc_ref)
    l_ref[...] = jnp.zeros_like(l_ref)
    m_ref[...] = jnp.full_like(m_ref, -jnp.inf)

    @pl.when(qi == 0)
    def _():
        group_tile(0, 0, [causal], True)
        for g in range(1, groups):
            group_tile(g, g - 1, [None, causal], True)

    @pl.when(qi > 0)
    def _():
        for g in range(groups):
            group_tile(g, first_diag + g - 1, [None, causal], True)

    def margin(g, n):
        bound = qk_bound + (c_first[g] - ce_ref[bi, h, jnp.maximum(n - 1, 0)])
        return bound - jnp.min(m_ref[pl.ds(g * blk, blk), :])

    def live(n):
        return functools.reduce(jnp.logical_or, [margin(g, n) > -UNDERFLOW_EXP2 for g in range(groups)])

    def safe():
        return functools.reduce(jnp.logical_and, [margin(g, first_diag + g - 1) < OVERFLOW_SAFE_EXP2
                                                  for g in range(groups)])

    def sweep(new_max, enabled):
        @pl.when(enabled & live(first_diag + groups - 2) & (qi > 0))
        def _():
            for g in range(1, groups):
                group_tile(g, first_diag - 1, [None] * g, new_max)

        if groups > 2:
            @pl.when(enabled & live(groups - 2) & (qi == 0))
            def _():
                for g in range(2, groups):
                    group_tile(g, 0, [None] * (g - 1), new_max)

        def body(state):
            n, _ = state
            for g in range(groups):
                group_tile(g, n - 2, [None, None], new_max)
            return n - 2, live(n - 2)

        n0 = jnp.maximum(first_diag - 1, 0)
        n, alive = lax.while_loop(lambda st: (st[0] >= 2) & st[1], body, (n0, enabled & live(n0)))

        @pl.when((n == 1) & alive)
        def _():
            for g in range(groups):
                group_tile(g, 0, [None], new_max)

    fixed_m = safe()
    sweep(False, fixed_m)
    sweep(True, jnp.logical_not(fixed_m))

    l = jnp.sum(l_ref[...], axis=-1, keepdims=True)
    o_ref[...] = (acc_ref[...] / l).astype(o_ref.dtype)


def _fox_attention(q, kv, c, d_model):
    b, s, _ = q.shape
    dh = d_model // N_HEADS
    blk = ATTN_BLOCK
    tq = ATTN_Q_TILE
    return pl.pallas_call(
        functools.partial(_fox_attn_kernel, blk=blk),
        out_shape=jax.ShapeDtypeStruct((b, s, d_model), BF16),
        grid=(b, N_HEADS, s // tq),
        in_specs=[pl.BlockSpec(memory_space=pltpu.SMEM),
                  pl.BlockSpec(memory_space=pltpu.SMEM),
                  pl.BlockSpec((None, tq, dh), lambda bi, h, qi: (bi, qi, h)),
                  _resident((None, s, dh), lambda bi, h, qi: (bi, 0, h)),
                  _resident((None, s, dh), lambda bi, h, qi: (bi, 0, N_HEADS + h)),
                  _resident((None, None, s // blk, blk), lambda bi, h, qi: (bi, h, 0, 0))],
        out_specs=pl.BlockSpec((None, tq, dh), lambda bi, h, qi: (bi, qi, h)),
        scratch_shapes=[pltpu.VMEM((tq, dh), F32), pltpu.VMEM((tq, 1), F32), pltpu.VMEM((tq, V7X_LANES), F32),
                        pltpu.SMEM((1,), F32)],
        compiler_params=_params("parallel", "parallel", "arbitrary"),
        name="fox_attention",
    )(c[..., 0], c[..., -1], q, kv, kv, c)


def _silu(g):
    return g / (1.0 + jnp.exp(-g))


def _ln_ple(x, ff, g, b, p, w_proj, w_gate, alpha):
    y = _layer_norm(alpha * x + ff, g, b)
    gate = 1.0 / (1.0 + jnp.exp(-_dot(y.astype(BF16), w_gate)))
    return y + _dot(p.astype(BF16), w_proj) * gate


def _mix_ln(x_ref, a_ref, wo_ref, g_ref, b_ref, alpha):
    return _layer_norm(alpha * x_ref[...] + _dot(a_ref[...], wo_ref[...]), g_ref[...], b_ref[...])


def _dense_ffn_kernel(x_ref, a_ref, wo_ref, g1_ref, b1_ref, wgu_ref, wd_ref, g_ref, b_ref, p_ref, wp_ref, wg_ref,
                      o_ref, *, alpha, d_ff):
    x = _mix_ln(x_ref, a_ref, wo_ref, g1_ref, b1_ref, alpha)
    h = _dot(x.astype(BF16), wgu_ref[...])
    act = (_silu(h[:, :d_ff]) * h[:, d_ff:]).astype(BF16)
    ff = _dot(act, wd_ref[...])
    o_ref[...] = _ln_ple(x, ff, g_ref[...], b_ref[...], p_ref[...], wp_ref[...], wg_ref[...], alpha)


def _dense_ffn(x, a, w_o, g1, b1, w_gu, w_down, g, b, p, layer, w_proj, w_gate, alpha):
    t, d = x.shape
    d_ff = w_down.shape[0]
    pd = p.shape[2]
    tm = ROW_BLOCK
    const = lambda i: (0, 0)
    return pl.pallas_call(
        functools.partial(_dense_ffn_kernel, alpha=alpha, d_ff=d_ff),
        out_shape=jax.ShapeDtypeStruct((t, d), F32),
        grid=(t // tm,),
        in_specs=[pl.BlockSpec((tm, d), lambda i: (i, 0)),
                  pl.BlockSpec((tm, d), lambda i: (i, 0)),
                  _resident((d, d), const),
                  _resident((1, d), const),
                  _resident((1, d), const),
                  _resident((d, 2 * d_ff), const),
                  _resident((d_ff, d), const),
                  _resident((1, d), const),
                  _resident((1, d), const),
                  pl.BlockSpec((None, tm, pd), lambda i: (layer, i, 0)),
                  _resident((pd, d), const),
                  _resident((d, d), const)],
        out_specs=pl.BlockSpec((tm, d), lambda i: (i, 0)),
        compiler_params=_params("parallel"),
        name="dense_ffn",
    )(x, a, w_o, g1, b1, w_gu, w_down, g, b, p, w_proj, w_gate)


def _top2_gates(logits):
    lane = lax.broadcasted_iota(jnp.int32, logits.shape, 1)
    v1 = jnp.max(logits, axis=-1, keepdims=True)
    i1 = jnp.min(jnp.where(logits == v1, lane, V7X_LANES), axis=-1, keepdims=True)
    rest = jnp.where(lane == i1, -jnp.inf, logits)
    v2 = jnp.max(rest, axis=-1, keepdims=True)
    i2 = jnp.min(jnp.where(rest == v2, lane, V7X_LANES), axis=-1, keepdims=True)
    e2 = jnp.exp(v2 - v1)
    g1 = 1.0 / (1.0 + e2)
    g2 = e2 / (1.0 + e2)
    return i1, g1, i2, g2


def _moe_ffn_kernel(x_ref, a_ref, wo_ref, g1_ref, b1_ref, wr_ref, wg_ref, wu_ref, wd_ref, g_ref, b_ref, p_ref,
                    wp_ref, wgate_ref, o_ref, acc_ref, route_ref, x1_ref, xb_ref, *, alpha, f_exp):
    pr = pl.program_id(1)
    tm = x_ref.shape[0]

    @pl.when(pr == 0)
    def _():
        x = _mix_ln(x_ref, a_ref, wo_ref, g1_ref, b1_ref, alpha)
        x1_ref[...] = x
        xb_ref[...] = x.astype(BF16)
        lane = lax.broadcasted_iota(jnp.int32, (tm, V7X_LANES), 1)
        logits = jnp.where(lane < N_EXPERTS, _hilo_dot(x, wr_ref[...]), -jnp.inf)
        i1, g1, i2, g2 = _top2_gates(logits)
        route_ref[...] = jnp.where(lane == i1, g1, 0.0) + jnp.where(lane == i2, g2, 0.0)
        acc_ref[...] = jnp.zeros_like(acc_ref)

    xb = xb_ref[...]
    act = _silu(_dot(xb, wg_ref[...])) * _dot(xb, wu_ref[...])
    lane = lax.broadcasted_iota(jnp.int32, (tm, V7X_LANES), 1)
    gates = route_ref[...]
    gate_a = jnp.sum(jnp.where(lane == 2 * pr, gates, 0.0), axis=-1, keepdims=True)
    gate_b = jnp.sum(jnp.where(lane == 2 * pr + 1, gates, 0.0), axis=-1, keepdims=True)
    lo = (f_exp // V7X_LANES) * V7X_LANES
    hi = lo + V7X_LANES
    mixed = jnp.where(lane < f_exp - lo, gate_a, gate_b)
    act = jnp.concatenate([act[:, :lo] * gate_a, act[:, lo:hi] * mixed, act[:, hi:] * gate_b], axis=1)
    acc_ref[...] += _dot(act.astype(BF16), wd_ref[...])

    @pl.when(pr == pl.num_programs(1) - 1)
    def _():
        o_ref[...] = _ln_ple(x1_ref[...], acc_ref[...], g_ref[...], b_ref[...], p_ref[...], wp_ref[...],
                             wgate_ref[...], alpha)


def _moe_ffn(x, a, w_o, g1, b1, w_router2, w_g, w_u, w_d, g, b, p, layer, w_proj, w_gate, alpha):
    t, d = x.shape
    f_exp = w_d.shape[0] // N_EXPERTS
    pd = p.shape[2]
    tm = ROW_BLOCK
    const = lambda i, pr: (0, 0)
    return pl.pallas_call(
        functools.partial(_moe_ffn_kernel, alpha=alpha, f_exp=f_exp),
        out_shape=jax.ShapeDtypeStruct((t, d), F32),
        grid=(t // tm, N_EXPERTS // 2),
        in_specs=[pl.BlockSpec((tm, d), lambda i, pr: (i, 0)),
                  pl.BlockSpec((tm, d), lambda i, pr: (i, 0)),
                  _resident((d, d), const),
                  _resident((1, d), const),
                  _resident((1, d), const),
                  _resident((d, V7X_LANES), const),
                  pl.BlockSpec((d, 2 * f_exp), lambda i, pr: (0, pr)),
                  pl.BlockSpec((d, 2 * f_exp), lambda i, pr: (0, pr)),
                  pl.BlockSpec((2 * f_exp, d), lambda i, pr: (pr, 0)),
                  _resident((1, d), const),
                  _resident((1, d), const),
                  pl.BlockSpec((None, tm, pd), lambda i, pr: (layer, i, 0)),
                  _resident((pd, d), const),
                  _resident((d, d), const)],
        out_specs=pl.BlockSpec((tm, d), lambda i, pr: (i, 0)),
        scratch_shapes=[pltpu.VMEM((tm, d), F32), pltpu.VMEM((tm, V7X_LANES), F32),
                        pltpu.VMEM((tm, d), F32), pltpu.VMEM((tm, d), BF16)],
        compiler_params=_params("parallel", "arbitrary"),
        name="moe_ffn",
    )(x, a, w_o, g1, b1, w_router2, w_g, w_u, w_d, g, b, p, w_proj, w_gate)


def kernel(x, p, a_w_qkv, a_w_o, kvf_w, kvf_b_f, b_w_q, b_w_o, ffn_w_gu, ffn_w_down, moe_w_router, moe_w_gu,
           moe_w_down, ple_w_proj, ple_w_gate, ln_g, ln_b):
    bsz, seq, d = x.shape
    depth = p.shape[0]
    n_a = a_w_qkv.shape[0]
    t = bsz * seq
    dh = d // N_HEADS
    alpha = (2.0 * depth) ** 0.25
    scale = LOG2E / math.sqrt(dh)
    f_exp = moe_w_down.shape[2]

    xf = x.reshape(t, d)
    pf = p.reshape(depth, t, p.shape[-1])

    c_sh = kv_sh = None
    for i in range(depth):
        g1, b1 = ln_g[i, 0][None, :], ln_b[i, 0][None, :]
        g2, b2 = ln_g[i, 1][None, :], ln_b[i, 1][None, :]
        if i < n_a:
            w_qkv = jnp.concatenate([a_w_qkv[i][:, :d] * scale, a_w_qkv[i][:, d:]], axis=1).astype(BF16)
            qkv = _proj(xf, w_qkv).reshape(bsz, seq, 3 * d)
            mix = _sb_attention(qkv, d).reshape(t, d)
            w_o = a_w_o[i].astype(BF16)
        else:
            if i == n_a:
                kv, f_logit = _proj_kvf(xf, kvf_w[:, :2 * d].astype(BF16), _hilo_weight(kvf_w[:, 2 * d:]))
                kv_sh = kv.reshape(bsz, seq, 2 * d)
                c_sh = _log_forget_cumsum(f_logit, kvf_b_f, bsz, seq)
                c_sh = c_sh.reshape(bsz, N_HEADS, seq // ATTN_BLOCK, ATTN_BLOCK)
            j = i - n_a
            q = _proj(xf, (b_w_q[j] * scale).astype(BF16)).reshape(bsz, seq, d)
            mix = _fox_attention(q, kv_sh, c_sh, d).reshape(t, d)
            w_o = b_w_o[j].astype(BF16)

        w_proj = ple_w_proj[i].astype(BF16)
        w_gate = ple_w_gate[i].astype(BF16)
        if i % 2 == 0:
            xf = _dense_ffn(xf, mix, w_o, g1, b1, ffn_w_gu[i // 2].astype(BF16), ffn_w_down[i // 2].astype(BF16),
                            g2, b2, pf, i, w_proj, w_gate, alpha)
        else:
            m = i // 2
            w_g = moe_w_gu[m][:, :, :f_exp].transpose(1, 0, 2).reshape(d, N_EXPERTS * f_exp).astype(BF16)
            w_u = moe_w_gu[m][:, :, f_exp:].transpose(1, 0, 2).reshape(d, N_EXPERTS * f_exp).astype(BF16)
            w_d = moe_w_down[m].reshape(N_EXPERTS * f_exp, d).astype(BF16)
            xf = _moe_ffn(xf, mix, w_o, g1, b1, _hilo_weight(moe_w_router[m]), w_g, w_u, w_d, g2, b2, pf, i,
                          w_proj, w_gate, alpha)
    return xf.reshape(bsz, seq, d)
```

```python
import functools
import math

import jax
import jax.numpy as jnp
from jax import lax
from jax.experimental import pallas as pl
from jax.experimental.pallas import tpu as pltpu

F32 = jnp.float32
BF16 = jnp.bfloat16

N_HEADS = 4
N_EXPERTS = 8
LN_EPS = 1e-5

V7X_LANES = 128
V7X_VMEM_LIMIT = 56 * 1024 * 1024

LOG2E = 1.4426950408889634
UNDERFLOW_EXP2 = 160.0
OVERFLOW_SAFE_EXP2 = 64.0

ROW_BLOCK = 512
ATTN_BLOCK = 256
ATTN_Q_TILE = 1024


def _params(*sem):
    return pltpu.CompilerParams(dimension_semantics=sem, vmem_limit_bytes=V7X_VMEM_LIMIT)


def _resident(shape, index_map):
    return pl.BlockSpec(shape, index_map, pipeline_mode=pl.Buffered(1))


def _dot(a, b):
    return jnp.dot(a, b, preferred_element_type=F32)


def _dot_nt(a, b):
    return lax.dot_general(a, b, (((1,), (1,)), ((), ())), preferred_element_type=F32)


def _split2(x):
    hi = x.astype(BF16)
    lo = (x - hi.astype(F32)).astype(BF16)
    return hi, lo


def _split3(x):
    a = x.astype(BF16)
    r = x - a.astype(F32)
    b = r.astype(BF16)
    c = (r - b.astype(F32)).astype(BF16)
    return a, b, c


def _hilo_weight(w):
    k, n = w.shape
    hi = w.astype(BF16)
    lo = (w - hi.astype(F32)).astype(BF16)
    out = jnp.zeros((k, V7X_LANES), BF16)
    out = out.at[:, :n].set(hi)
    out = out.at[:, 64:64 + n].set(lo)
    return out


def _hilo_dot(x, w2):
    xh, xl = _split2(x)
    a = _dot(xh, w2)
    b = _dot(xl, w2)
    return a + pltpu.roll(a, 64, 1) + b


def _layer_norm(y, g, b):
    mu = jnp.mean(y, axis=-1, keepdims=True)
    d = y - mu
    var = jnp.mean(d * d, axis=-1, keepdims=True)
    return d * lax.rsqrt(var + LN_EPS) * g + b


def _proj_kernel(x_ref, w_ref, o_ref):
    o_ref[...] = _dot(x_ref[...].astype(BF16), w_ref[...]).astype(o_ref.dtype)


def _proj(x, w):
    t, k = x.shape
    n = w.shape[1]
    return pl.pallas_call(
        _proj_kernel,
        out_shape=jax.ShapeDtypeStruct((t, n), BF16),
        grid=(t // ROW_BLOCK,),
        in_specs=[pl.BlockSpec((ROW_BLOCK, k), lambda i: (i, 0)),
                  _resident((k, n), lambda i: (0, 0))],
        out_specs=pl.BlockSpec((ROW_BLOCK, n), lambda i: (i, 0)),
        compiler_params=_params("parallel"),
        name="proj",
    )(x, w)


def _proj_kvf_kernel(x_ref, w_ref, wf_ref, o_ref, f_ref, kn_ref, *, dh):
    x = x_ref[...]
    kv = _dot(x.astype(BF16), w_ref[...])
    o_ref[...] = kv.astype(o_ref.dtype)
    f_ref[...] = _hilo_dot(x, wf_ref[...])
    lane = lax.broadcasted_iota(jnp.int32, kn_ref.shape, 1)
    best = jnp.zeros(kn_ref.shape, F32)
    for h in range(N_HEADS):
        kh = kv[:, h * dh:(h + 1) * dh]
        best = jnp.where(lane == h, jnp.max(jnp.sum(kh * kh, axis=-1, keepdims=True)), best)
    kn_ref[...] = best


def _proj_kvf(x, w, wf2):
    t, k = x.shape
    n = w.shape[1]
    nblk = t // ROW_BLOCK
    return pl.pallas_call(
        functools.partial(_proj_kvf_kernel, dh=n // (2 * N_HEADS)),
        out_shape=(jax.ShapeDtypeStruct((t, n), BF16),
                   jax.ShapeDtypeStruct((t, V7X_LANES), F32),
                   jax.ShapeDtypeStruct((nblk, 8, V7X_LANES), F32)),
        grid=(nblk,),
        in_specs=[pl.BlockSpec((ROW_BLOCK, k), lambda i: (i, 0)),
                  _resident((k, n), lambda i: (0, 0)),
                  _resident((k, V7X_LANES), lambda i: (0, 0))],
        out_specs=(pl.BlockSpec((ROW_BLOCK, n), lambda i: (i, 0)),
                   pl.BlockSpec((ROW_BLOCK, V7X_LANES), lambda i: (i, 0)),
                   pl.BlockSpec((None, 8, V7X_LANES), lambda i: (i, 0, 0))),
        compiler_params=_params("parallel"),
        name="proj_kvf",
    )(x, w, wf2)


def _cumsum_kernel(v_ref, bias_ref, c_ref, *, rows_per_seq):
    z = v_ref[...] + bias_ref[...]
    a = (jnp.minimum(z, 0.0) - jnp.log(1.0 + jnp.exp(-jnp.abs(z)))) * LOG2E
    r = a.shape[0]
    j = lax.broadcasted_iota(jnp.int32, (V7X_LANES, V7X_LANES), 0)
    s = lax.broadcasted_iota(jnp.int32, (V7X_LANES, V7X_LANES), 1)
    upper = jnp.where(j <= s, 1.0, 0.0).astype(BF16)
    a1, a2, a3 = _split3(a)
    within = _dot(a1, upper) + _dot(a2, upper) + _dot(a3, upper)
    ri = lax.broadcasted_iota(jnp.int32, (r, r), 0)
    rj = lax.broadcasted_iota(jnp.int32, (r, r), 1)
    same = (ri // rows_per_seq) == (rj // rows_per_seq)
    before = jnp.where(same & (rj < ri), 1.0, 0.0).astype(BF16)
    w1, w2, w3 = _split3(within)
    offs = _dot(before, w1) + _dot(before, w2) + _dot(before, w3)
    c_ref[...] = within + offs[:, V7X_LANES - 1:V7X_LANES]


def _log_forget_cumsum(f_logit, bias_f, bsz, seq):
    rows_per_seq = seq // V7X_LANES
    fl = f_logit.reshape(bsz, seq, V7X_LANES)[:, :, :N_HEADS].transpose(0, 2, 1)
    fl = fl.reshape(bsz * N_HEADS * rows_per_seq, V7X_LANES)
    bias = jnp.broadcast_to(bias_f.astype(F32)[None, :, None, None],
                            (bsz, N_HEADS, rows_per_seq, 1)).reshape(-1, 1)
    c = pl.pallas_call(
        functools.partial(_cumsum_kernel, rows_per_seq=rows_per_seq),
        out_shape=jax.ShapeDtypeStruct(fl.shape, F32),
        compiler_params=_params(),
        name="forget_cumsum",
    )(fl, bias)
    return c.reshape(bsz, N_HEADS, seq)


def _softplus2(z):
    return jnp.maximum(z, 0.0) + jnp.log(1.0 + jnp.exp2(-jnp.abs(z))) * LOG2E


def _sb_attn_kernel(q_ref, k_ref, v_ref, o_ref, acc_ref, carry_ref, *, blk):
    qi = pl.program_id(2)
    groups = q_ref.shape[0] // blk
    first_diag = qi * groups
    row = lax.broadcasted_iota(jnp.int32, (blk, blk), 0)
    col = lax.broadcasted_iota(jnp.int32, (blk, blk), 1)
    causal = col < row
    suffix = jnp.where(row >= col, 1.0, 0.0).astype(BF16)

    def group_tile(g, first_blk, masks):
        n = len(masks)
        rows = pl.ds(g * blk, blk)
        start = pl.multiple_of(first_blk * blk, blk)
        k = k_ref[pl.ds(start, n * blk), :]
        v = v_ref[pl.ds(start, n * blk), :]
        z = _dot_nt(q_ref[rows, :], k)
        later = carry_ref[rows, :]
        ps = [None] * n
        for i in reversed(range(n)):
            zi = z[:, i * blk:(i + 1) * blk]
            sp = _softplus2(zi)
            if masks[i] is not None:
                sp = jnp.where(masks[i], sp, 0.0)
            r = _dot(sp.astype(BF16), suffix) + later
            p = jnp.exp2(zi - r)
            if masks[i] is not None:
                p = jnp.where(masks[i], p, 0.0)
            ps[i] = p.astype(BF16)
            later = r[:, 0:1]
        acc_ref[rows, :] += _dot(jnp.concatenate(ps, axis=1) if n > 1 else ps[0], v)
        carry_ref[rows, :] = later

    acc_ref[...] = jnp.zeros_like(acc_ref)
    carry_ref[...] = jnp.zeros_like(carry_ref)

    @pl.when(qi == 0)
    def _():
        group_tile(0, 0, [causal])
        for g in range(1, groups):
            group_tile(g, g - 1, [None, causal])

    @pl.when(qi > 0)
    def _():
        for g in range(groups):
            group_tile(g, first_diag + g - 1, [None, causal])

    def live():
        return jnp.min(carry_ref[...]) < UNDERFLOW_EXP2

    @pl.when(live())
    def _():
        @pl.when(qi > 0)
        def _():
            for g in range(1, groups):
                group_tile(g, first_diag - 1, [None] * g)

        if groups > 2:
            @pl.when(qi == 0)
            def _():
                for g in range(2, groups):
                    group_tile(g, 0, [None] * (g - 1))

        def body(state):
            n, _ = state
            for g in range(groups):
                group_tile(g, n - 2, [None, None])
            return n - 2, live()

        n, alive = lax.while_loop(lambda st: (st[0] >= 2) & st[1], body,
                                  (jnp.maximum(first_diag - 1, 0), live()))

        @pl.when((n == 1) & alive)
        def _():
            for g in range(groups):
                group_tile(g, 0, [None])

    o_ref[...] = acc_ref[...].astype(o_ref.dtype)


def _sb_attention(qkv, d_model):
    b, s, _ = qkv.shape
    dh = d_model // N_HEADS
    tq = ATTN_Q_TILE
    return pl.pallas_call(
        functools.partial(_sb_attn_kernel, blk=ATTN_BLOCK),
        out_shape=jax.ShapeDtypeStruct((b, s, d_model), BF16),
        grid=(b, N_HEADS, s // tq),
        in_specs=[pl.BlockSpec((None, tq, dh), lambda bi, h, qi: (bi, qi, h)),
                  _resident((None, s, dh), lambda bi, h, qi: (bi, 0, N_HEADS + h)),
                  _resident((None, s, dh), lambda bi, h, qi: (bi, 0, 2 * N_HEADS + h))],
        out_specs=pl.BlockSpec((None, tq, dh), lambda bi, h, qi: (bi, qi, h)),
        scratch_shapes=[pltpu.VMEM((tq, dh), F32), pltpu.VMEM((tq, 1), F32)],
        compiler_params=_params("parallel", "parallel", "arbitrary"),
        name="sb_attention",
    )(qkv, qkv, qkv)


def _row_norm_max(x):
    xf = x.astype(F32)
    return jnp.max(jnp.sqrt(jnp.sum(xf * xf, axis=-1, keepdims=True)))


def _fox_attn_kernel(cs_ref, ce_ref, kn_ref, q_ref, k_ref, v_ref, c_ref, o_ref, acc_ref, m_ref, l_ref, *, blk):
    bi, h, qi = pl.program_id(0), pl.program_id(1), pl.program_id(2)
    q = q_ref[...]
    groups = q.shape[0] // blk
    first_diag = qi * groups
    row = lax.broadcasted_iota(jnp.int32, (blk, blk), 0)
    col = lax.broadcasted_iota(jnp.int32, (blk, blk), 1)
    causal = col <= row

    qk_bound = _row_norm_max(q) * kn_ref[bi, h]
    c_first = [cs_ref[bi, h, first_diag + g] for g in range(groups)]

    def group_tile(g, first_blk, masks, new_max):
        n = len(masks)
        rows = pl.ds(g * blk, blk)
        start = pl.multiple_of(first_blk * blk, blk)
        k = k_ref[pl.ds(start, n * blk), :]
        v = v_ref[pl.ds(start, n * blk), :]
        z = _dot_nt(q_ref[rows, :], k)
        zs = []
        for i in range(n):
            zi = z[:, i * blk:(i + 1) * blk] + (c_first[g] - c_ref[pl.ds(first_blk + i, 1), :])
            if masks[i] is not None:
                zi = jnp.where(masks[i], zi, -jnp.inf)
            zs.append(zi)
        z = jnp.concatenate(zs, axis=1) if n > 1 else zs[0]
        m_old = m_ref[rows, :]
        if new_max:
            m_new = jnp.maximum(m_old, jnp.max(z, axis=-1, keepdims=True))
            m_ref[rows, :] = m_new
        else:
            m_new = m_old
        p = jnp.exp2(z - m_new)
        lanes = sum(p[:, j * V7X_LANES:(j + 1) * V7X_LANES] for j in range(p.shape[1] // V7X_LANES))
        pv = _dot(p.astype(BF16), v)
        if new_max:
            alpha = jnp.exp2(m_old - m_new)
            l_ref[rows, :] = alpha * l_ref[rows, :] + lanes
            acc_ref[rows, :] = alpha * acc_ref[rows, :] + pv
        else:
            l_ref[rows, :] += lanes
            acc_ref[rows, :] += pv

    acc_ref[...] = jnp.zeros_like(acc_ref)
    l_ref[...] = jnp.zeros_like(l_ref)
    m_ref[...] = jnp.full_like(m_ref, -jnp.inf)

    @pl.when(qi == 0)
    def _():
        group_tile(0, 0, [causal], True)
        for g in range(1, groups):
            group_tile(g, g - 1, [None, causal], True)

    @pl.when(qi > 0)
    def _():
        for g in range(groups):
            group_tile(g, first_diag + g - 1, [None, causal], True)

    def per_row(scalars):
        return jnp.concatenate([jnp.full((blk, 1), s, F32) for s in scalars], axis=0)

    c_first_rows = per_row(c_first)

    def row_slack():
        return jnp.max(c_first_rows - m_ref[...])

    def live(n, slack):
        return qk_bound + slack - ce_ref[bi, h, jnp.maximum(n - 1, 0)] > -UNDERFLOW_EXP2

    c_end_rows = per_row([ce_ref[bi, h, jnp.maximum(first_diag + g - 2, 0)] for g in range(groups)])
    fixed_m = qk_bound + jnp.max(c_first_rows - c_end_rows - m_ref[...]) < OVERFLOW_SAFE_EXP2
    slack0 = row_slack()

    def sweep(new_max):
        def slack_now():
            return row_slack() if new_max else slack0

        @pl.when(live(first_diag + groups - 2, slack0) & (qi > 0))
        def _():
            for g in range(1, groups):
                group_tile(g, first_diag - 1, [None] * g, new_max)

        if groups > 2:
            @pl.when(live(groups - 2, slack0) & (qi == 0))
            def _():
                for g in range(2, groups):
                    group_tile(g, 0, [None] * (g - 1), new_max)

        def body(state):
            n, _ = state
            for g in range(groups):
                group_tile(g, n - 2, [None, None], new_max)
            return n - 2, live(n - 2, slack_now())

        n0 = jnp.maximum(first_diag - 1, 0)
        n, alive = lax.while_loop(lambda st: (st[0] >= 2) & st[1], body, (n0, live(n0, slack_now())))

        @pl.when((n == 1) & alive)
        def _():
            for g in range(groups):
                group_tile(g, 0, [None], new_max)

    @pl.when(fixed_m)
    def _():
        sweep(False)

    @pl.when(jnp.logical_not(fixed_m))
    def _():
        sweep(True)

    l = jnp.sum(l_ref[...], axis=-1, keepdims=True)
    o_ref[...] = (acc_ref[...] / l).astype(o_ref.dtype)


def _fox_attention(q, kv, c, key_norm, d_model):
    b, s, _ = q.shape
    dh = d_model // N_HEADS
    blk = ATTN_BLOCK
    tq = ATTN_Q_TILE
    return pl.pallas_call(
        functools.partial(_fox_attn_kernel, blk=blk),
        out_shape=jax.ShapeDtypeStruct((b, s, d_model), BF16),
        grid=(b, N_HEADS, s // tq),
        in_specs=[pl.BlockSpec(memory_space=pltpu.SMEM),
                  pl.BlockSpec(memory_space=pltpu.SMEM),
                  pl.BlockSpec(memory_space=pltpu.SMEM),
                  pl.BlockSpec((None, tq, dh), lambda bi, h, qi: (bi, qi, h)),
                  _resident((None, s, dh), lambda bi, h, qi: (bi, 0, h)),
                  _resident((None, s, dh), lambda bi, h, qi: (bi, 0, N_HEADS + h)),
                  _resident((None, None, s // blk, blk), lambda bi, h, qi: (bi, h, 0, 0))],
        out_specs=pl.BlockSpec((None, tq, dh), lambda bi, h, qi: (bi, qi, h)),
        scratch_shapes=[pltpu.VMEM((tq, dh), F32), pltpu.VMEM((tq, 1), F32), pltpu.VMEM((tq, V7X_LANES), F32)],
        compiler_params=_params("parallel", "parallel", "arbitrary"),
        name="fox_attention",
    )(c[..., 0], c[..., -1], key_norm, q, kv, kv, c)


def _silu(g):
    return g / (1.0 + jnp.exp(-g))


def _ln_ple(x, ff, g, b, p, w_proj, w_gate, alpha):
    y = _layer_norm(alpha * x + ff, g, b)
    gate = 1.0 / (1.0 + jnp.exp(-_dot(y.astype(BF16), w_gate)))
    return y + _dot(p.astype(BF16), w_proj) * gate


def _mix_ln(x_ref, a_ref, wo_ref, g_ref, b_ref, alpha):
    return _layer_norm(alpha * x_ref[...] + _dot(a_ref[...], wo_ref[...]), g_ref[...], b_ref[...])


def _dense_ffn_kernel(x_ref, a_ref, wo_ref, g1_ref, b1_ref, wgu_ref, wd_ref, g_ref, b_ref, p_ref, wp_ref, wg_ref,
                      o_ref, *, alpha, d_ff):
    x = _mix_ln(x_ref, a_ref, wo_ref, g1_ref, b1_ref, alpha)
    h = _dot(x.astype(BF16), wgu_ref[...])
    act = (_silu(h[:, :d_ff]) * h[:, d_ff:]).astype(BF16)
    ff = _dot(act, wd_ref[...])
    o_ref[...] = _ln_ple(x, ff, g_ref[...], b_ref[...], p_ref[...], wp_ref[...], wg_ref[...], alpha)


def _dense_ffn(x, a, w_o, g1, b1, w_gu, w_down, g, b, p, layer, w_proj, w_gate, alpha):
    t, d = x.shape
    d_ff = w_down.shape[0]
    pd = p.shape[2]
    tm = ROW_BLOCK
    const = lambda i: (0, 0)
    return pl.pallas_call(
        functools.partial(_dense_ffn_kernel, alpha=alpha, d_ff=d_ff),
        out_shape=jax.ShapeDtypeStruct((t, d), F32),
        grid=(t // tm,),
        in_specs=[pl.BlockSpec((tm, d), lambda i: (i, 0)),
                  pl.BlockSpec((tm, d), lambda i: (i, 0)),
                  _resident((d, d), const),
                  _resident((1, d), const),
                  _resident((1, d), const),
                  _resident((d, 2 * d_ff), const),
                  _resident((d_ff, d), const),
                  _resident((1, d), const),
                  _resident((1, d), const),
                  pl.BlockSpec((None, tm, pd), lambda i: (layer, i, 0)),
                  _resident((pd, d), const),
                  _resident((d, d), const)],
        out_specs=pl.BlockSpec((tm, d), lambda i: (i, 0)),
        compiler_params=_params("parallel"),
        name="dense_ffn",
    )(x, a, w_o, g1, b1, w_gu, w_down, g, b, p, w_proj, w_gate)


def _top2_gates(logits):
    lane = lax.broadcasted_iota(jnp.int32, logits.shape, 1)
    v1 = jnp.max(logits, axis=-1, keepdims=True)
    i1 = jnp.min(jnp.where(logits == v1, lane, V7X_LANES), axis=-1, keepdims=True)
    rest = jnp.where(lane == i1, -jnp.inf, logits)
    v2 = jnp.max(rest, axis=-1, keepdims=True)
    i2 = jnp.min(jnp.where(rest == v2, lane, V7X_LANES), axis=-1, keepdims=True)
    e2 = jnp.exp(v2 - v1)
    g1 = 1.0 / (1.0 + e2)
    g2 = e2 / (1.0 + e2)
    return i1, g1, i2, g2


def _moe_ffn_kernel(x_ref, a_ref, wo_ref, g1_ref, b1_ref, wr_ref, wg_ref, wu_ref, wd_ref, g_ref, b_ref, p_ref,
                    wp_ref, wgate_ref, o_ref, acc_ref, route_ref, x1_ref, xb_ref, *, alpha, f_exp):
    pr = pl.program_id(1)
    tm = x_ref.shape[0]

    @pl.when(pr == 0)
    def _():
        x = _mix_ln(x_ref, a_ref, wo_ref, g1_ref, b1_ref, alpha)
        x1_ref[...] = x
        xb_ref[...] = x.astype(BF16)
        lane = lax.broadcasted_iota(jnp.int32, (tm, V7X_LANES), 1)
        logits = jnp.where(lane < N_EXPERTS, _hilo_dot(x, wr_ref[...]), -jnp.inf)
        i1, g1, i2, g2 = _top2_gates(logits)
        route_ref[...] = jnp.where(lane == i1, g1, 0.0) + jnp.where(lane == i2, g2, 0.0)
        acc_ref[...] = jnp.zeros_like(acc_ref)

    xb = xb_ref[...]
    act = _silu(_dot(xb, wg_ref[...])) * _dot(xb, wu_ref[...])
    lane = lax.broadcasted_iota(jnp.int32, (tm, V7X_LANES), 1)
    gates = route_ref[...]
    gate_a = jnp.sum(jnp.where(lane == 2 * pr, gates, 0.0), axis=-1, keepdims=True)
    gate_b = jnp.sum(jnp.where(lane == 2 * pr + 1, gates, 0.0), axis=-1, keepdims=True)
    lo = (f_exp // V7X_LANES) * V7X_LANES
    hi = lo + V7X_LANES
    mixed = jnp.where(lane < f_exp - lo, gate_a, gate_b)
    act = jnp.concatenate([act[:, :lo] * gate_a, act[:, lo:hi] * mixed, act[:, hi:] * gate_b], axis=1)
    acc_ref[...] += _dot(act.astype(BF16), wd_ref[...])

    @pl.when(pr == pl.num_programs(1) - 1)
    def _():
        o_ref[...] = _ln_ple(x1_ref[...], acc_ref[...], g_ref[...], b_ref[...], p_ref[...], wp_ref[...],
                             wgate_ref[...], alpha)


def _moe_ffn(x, a, w_o, g1, b1, w_router2, w_g, w_u, w_d, g, b, p, layer, w_proj, w_gate, alpha):
    t, d = x.shape
    f_exp = w_d.shape[0] // N_EXPERTS
    pd = p.shape[2]
    tm = ROW_BLOCK
    const = lambda i, pr: (0, 0)
    return pl.pallas_call(
        functools.partial(_moe_ffn_kernel, alpha=alpha, f_exp=f_exp),
        out_shape=jax.ShapeDtypeStruct((t, d), F32),
        grid=(t // tm, N_EXPERTS // 2),
        in_specs=[pl.BlockSpec((tm, d), lambda i, pr: (i, 0)),
                  pl.BlockSpec((tm, d), lambda i, pr: (i, 0)),
                  _resident((d, d), const),
                  _resident((1, d), const),
                  _resident((1, d), const),
                  _resident((d, V7X_LANES), const),
                  pl.BlockSpec((d, 2 * f_exp), lambda i, pr: (0, pr)),
                  pl.BlockSpec((d, 2 * f_exp), lambda i, pr: (0, pr)),
                  pl.BlockSpec((2 * f_exp, d), lambda i, pr: (pr, 0)),
                  _resident((1, d), const),
                  _resident((1, d), const),
                  pl.BlockSpec((None, tm, pd), lambda i, pr: (layer, i, 0)),
                  _resident((pd, d), const),
                  _resident((d, d), const)],
        out_specs=pl.BlockSpec((tm, d), lambda i, pr: (i, 0)),
        scratch_shapes=[pltpu.VMEM((tm, d), F32), pltpu.VMEM((tm, V7X_LANES), F32),
                        pltpu.VMEM((tm, d), F32), pltpu.VMEM((tm, d), BF16)],
        compiler_params=_params("parallel", "arbitrary"),
        name="moe_ffn",
    )(x, a, w_o, g1, b1, w_router2, w_g, w_u, w_d, g, b, p, w_proj, w_gate)


def kernel(x, p, a_w_qkv, a_w_o, kvf_w, kvf_b_f, b_w_q, b_w_o, ffn_w_gu, ffn_w_down, moe_w_router, moe_w_gu,
           moe_w_down, ple_w_proj, ple_w_gate, ln_g, ln_b):
    bsz, seq, d = x.shape
    depth = p.shape[0]
    n_a = a_w_qkv.shape[0]
    t = bsz * seq
    dh = d // N_HEADS
    alpha = (2.0 * depth) ** 0.25
    scale = LOG2E / math.sqrt(dh)
    f_exp = moe_w_down.shape[2]

    xf = x.reshape(t, d)
    pf = p.reshape(depth, t, p.shape[-1])

    c_sh = kv_sh = kn_sh = None
    for i in range(depth):
        g1, b1 = ln_g[i, 0][None, :], ln_b[i, 0][None, :]
        g2, b2 = ln_g[i, 1][None, :], ln_b[i, 1][None, :]
        if i < n_a:
            w_qkv = jnp.concatenate([a_w_qkv[i][:, :d] * scale, a_w_qkv[i][:, d:]], axis=1).astype(BF16)
            qkv = _proj(xf, w_qkv).reshape(bsz, seq, 3 * d)
            mix = _sb_attention(qkv, d).reshape(t, d)
            w_o = a_w_o[i].astype(BF16)
        else:
            if i == n_a:
                kv, f_logit, kn2 = _proj_kvf(xf, kvf_w[:, :2 * d].astype(BF16), _hilo_weight(kvf_w[:, 2 * d:]))
                kv_sh = kv.reshape(bsz, seq, 2 * d)
                kn_sh = 1.02 * jnp.sqrt(jnp.max(kn2[:, 0, :N_HEADS].reshape(bsz, -1, N_HEADS), axis=1))
                c_sh = _log_forget_cumsum(f_logit, kvf_b_f, bsz, seq)
                c_sh = c_sh.reshape(bsz, N_HEADS, seq // ATTN_BLOCK, ATTN_BLOCK)
            j = i - n_a
            q = _proj(xf, (b_w_q[j] * scale).astype(BF16)).reshape(bsz, seq, d)
            mix = _fox_attention(q, kv_sh, c_sh, kn_sh, d).reshape(t, d)
            w_o = b_w_o[j].astype(BF16)

        w_proj = ple_w_proj[i].astype(BF16)
        w_gate = ple_w_gate[i].astype(BF16)
        if i % 2 == 0:
            xf = _dense_ffn(xf, mix, w_o, g1, b1, ffn_w_gu[i // 2].astype(BF16), ffn_w_down[i // 2].astype(BF16),
                            g2, b2, pf, i, w_proj, w_gate, alpha)
        else:
            m = i // 2
            w_g = moe_w_gu[m][:, :, :f_exp].transpose(1, 0, 2).reshape(d, N_EXPERTS * f_exp).astype(BF16)
            w_u = moe_w_gu[m][:, :, f_exp:].transpose(1, 0, 2).reshape(d, N_EXPERTS * f_exp).astype(BF16)
            w_d = moe_w_down[m].reshape(N_EXPERTS * f_exp, d).astype(BF16)
            xf = _moe_ffn(xf, mix, w_o, g1, b1, _hilo_weight(moe_w_router[m]), w_g, w_u, w_d, g2, b2, pf, i,
                          w_proj, w_gate, alpha)
    return xf.reshape(bsz, seq, d)
```

```python
import functools
import math

import jax
import jax.numpy as jnp
from jax import lax
from jax.experimental import pallas as pl
from jax.experimental.pallas import tpu as pltpu

F32 = jnp.float32
BF16 = jnp.bfloat16

N_HEADS = 4
N_EXPERTS = 8
LN_EPS = 1e-5

V7X_LANES = 128
V7X_VMEM_LIMIT = 56 * 1024 * 1024

LOG2E = 1.4426950408889634
UNDERFLOW_EXP2 = 160.0
OVERFLOW_SAFE_EXP2 = 64.0

ROW_BLOCK = 512
ATTN_BLOCK = 256
ATTN_Q_TILE = 1024


def _params(*sem):
    return pltpu.CompilerParams(dimension_semantics=sem, vmem_limit_bytes=V7X_VMEM_LIMIT)


def _resident(shape, index_map):
    return pl.BlockSpec(shape, index_map, pipeline_mode=pl.Buffered(1))


def _dot(a, b):
    return jnp.dot(a, b, preferred_element_type=F32)


def _dot_nt(a, b):
    return lax.dot_general(a, b, (((1,), (1,)), ((), ())), preferred_element_type=F32)


def _split2(x):
    hi = x.astype(BF16)
    lo = (x - hi.astype(F32)).astype(BF16)
    return hi, lo


def _split3(x):
    a = x.astype(BF16)
    r = x - a.astype(F32)
    b = r.astype(BF16)
    c = (r - b.astype(F32)).astype(BF16)
    return a, b, c


def _hilo_weight(w):
    k, n = w.shape
    hi = w.astype(BF16)
    lo = (w - hi.astype(F32)).astype(BF16)
    out = jnp.zeros((k, V7X_LANES), BF16)
    out = out.at[:, :n].set(hi)
    out = out.at[:, 64:64 + n].set(lo)
    return out


def _hilo_dot(x, w2):
    xh, xl = _split2(x)
    a = _dot(xh, w2)
    b = _dot(xl, w2)
    return a + pltpu.roll(a, 64, 1) + b


def _layer_norm(y, g, b):
    mu = jnp.mean(y, axis=-1, keepdims=True)
    d = y - mu
    var = jnp.mean(d * d, axis=-1, keepdims=True)
    return d * lax.rsqrt(var + LN_EPS) * g + b


def _proj_kernel(x_ref, w_ref, o_ref):
    o_ref[...] = _dot(x_ref[...].astype(BF16), w_ref[...]).astype(o_ref.dtype)


def _proj(x, w):
    t, k = x.shape
    n = w.shape[1]
    return pl.pallas_call(
        _proj_kernel,
        out_shape=jax.ShapeDtypeStruct((t, n), BF16),
        grid=(t // ROW_BLOCK,),
        in_specs=[pl.BlockSpec((ROW_BLOCK, k), lambda i: (i, 0)),
                  _resident((k, n), lambda i: (0, 0))],
        out_specs=pl.BlockSpec((ROW_BLOCK, n), lambda i: (i, 0)),
        compiler_params=_params("parallel"),
        name="proj",
    )(x, w)


def _proj_kvf_kernel(x_ref, w_ref, wf_ref, o_ref, f_ref, kn_ref, *, dh):
    x = x_ref[...]
    kv = _dot(x.astype(BF16), w_ref[...])
    o_ref[...] = kv.astype(o_ref.dtype)
    xh, xl = _split2(x)
    a = _dot_nt(wf_ref[...], xh)
    b = _dot_nt(wf_ref[...], xl)
    f_ref[...] = a[0:8, :] + a[64:72, :] + b[0:8, :]
    lane = lax.broadcasted_iota(jnp.int32, kn_ref.shape, 1)
    best = jnp.zeros(kn_ref.shape, F32)
    for h in range(N_HEADS):
        kh = kv[:, h * dh:(h + 1) * dh]
        best = jnp.where(lane == h, jnp.max(jnp.sum(kh * kh, axis=-1, keepdims=True)), best)
    kn_ref[...] = best


def _proj_kvf(x, w, wf2t):
    t, k = x.shape
    n = w.shape[1]
    nblk = t // ROW_BLOCK
    return pl.pallas_call(
        functools.partial(_proj_kvf_kernel, dh=n // (2 * N_HEADS)),
        out_shape=(jax.ShapeDtypeStruct((t, n), BF16),
                   jax.ShapeDtypeStruct((8, t), F32),
                   jax.ShapeDtypeStruct((nblk, 8, V7X_LANES), F32)),
        grid=(nblk,),
        in_specs=[pl.BlockSpec((ROW_BLOCK, k), lambda i: (i, 0)),
                  _resident((k, n), lambda i: (0, 0)),
                  _resident((V7X_LANES, k), lambda i: (0, 0))],
        out_specs=(pl.BlockSpec((ROW_BLOCK, n), lambda i: (i, 0)),
                   pl.BlockSpec((8, ROW_BLOCK), lambda i: (0, i)),
                   pl.BlockSpec((None, 8, V7X_LANES), lambda i: (i, 0, 0))),
        compiler_params=_params("parallel"),
        name="proj_kvf",
    )(x, w, wf2t)


def _cumsum_kernel(v_ref, bias_ref, c_ref, *, rows_per_seq):
    z = v_ref[...] + bias_ref[...]
    a = (jnp.minimum(z, 0.0) - jnp.log(1.0 + jnp.exp(-jnp.abs(z)))) * LOG2E
    r = a.shape[0]
    j = lax.broadcasted_iota(jnp.int32, (V7X_LANES, V7X_LANES), 0)
    s = lax.broadcasted_iota(jnp.int32, (V7X_LANES, V7X_LANES), 1)
    upper = jnp.where(j <= s, 1.0, 0.0).astype(BF16)
    a1, a2, a3 = _split3(a)
    within = _dot(a1, upper) + _dot(a2, upper) + _dot(a3, upper)
    ri = lax.broadcasted_iota(jnp.int32, (r, r), 0)
    rj = lax.broadcasted_iota(jnp.int32, (r, r), 1)
    same = (ri // rows_per_seq) == (rj // rows_per_seq)
    before = jnp.where(same & (rj < ri), 1.0, 0.0).astype(BF16)
    w1, w2, w3 = _split3(within)
    offs = _dot(before, w1) + _dot(before, w2) + _dot(before, w3)
    c_ref[...] = within + offs[:, V7X_LANES - 1:V7X_LANES]


def _log_forget_cumsum(f_logit_t, bias_f, bsz, seq):
    rows_per_seq = seq // V7X_LANES
    fl = f_logit_t[:N_HEADS].reshape(N_HEADS * bsz * rows_per_seq, V7X_LANES)
    bias = jnp.broadcast_to(bias_f.astype(F32)[:, None, None, None],
                            (N_HEADS, bsz, rows_per_seq, 1)).reshape(-1, 1)
    c = pl.pallas_call(
        functools.partial(_cumsum_kernel, rows_per_seq=rows_per_seq),
        out_shape=jax.ShapeDtypeStruct(fl.shape, F32),
        compiler_params=_params(),
        name="forget_cumsum",
    )(fl, bias)
    return c.reshape(N_HEADS, bsz, seq).transpose(1, 0, 2)


def _softplus2(z):
    return jnp.maximum(z, 0.0) + jnp.log(1.0 + jnp.exp2(-jnp.abs(z))) * LOG2E


def _sb_attn_kernel(q_ref, k_ref, v_ref, o_ref, acc_ref, carry_ref, *, blk):
    qi = pl.program_id(2)
    groups = q_ref.shape[0] // blk
    first_diag = qi * groups
    row = lax.broadcasted_iota(jnp.int32, (blk, blk), 0)
    col = lax.broadcasted_iota(jnp.int32, (blk, blk), 1)
    causal = col < row
    suffix = jnp.where(row >= col, 1.0, 0.0).astype(BF16)

    def group_tile(g, first_blk, masks, first=False):
        n = len(masks)
        rows = pl.ds(g * blk, blk)
        start = pl.multiple_of(first_blk * blk, blk)
        k = k_ref[pl.ds(start, n * blk), :]
        v = v_ref[pl.ds(start, n * blk), :]
        z = _dot_nt(q_ref[rows, :], k)
        later = None if first else carry_ref[rows, :]
        ps = [None] * n
        for i in reversed(range(n)):
            zi = z[:, i * blk:(i + 1) * blk]
            if masks[i] is not None:
                zi = jnp.where(masks[i], zi, -jnp.inf)
            sp = _softplus2(zi)
            r = _dot(sp.astype(BF16), suffix)
            if later is not None:
                r = r + later
            ps[i] = jnp.exp2(zi - r).astype(BF16)
            later = r[:, 0:1]
        pv = _dot(jnp.concatenate(ps, axis=1) if n > 1 else ps[0], v)
        if first:
            acc_ref[rows, :] = pv
        else:
            acc_ref[rows, :] += pv
        carry_ref[rows, :] = later

    @pl.when(qi == 0)
    def _():
        group_tile(0, 0, [causal], first=True)
        for g in range(1, groups):
            group_tile(g, g - 1, [None, causal], first=True)

    @pl.when(qi > 0)
    def _():
        for g in range(groups):
            group_tile(g, first_diag + g - 1, [None, causal], first=True)

    def live():
        return jnp.min(carry_ref[...]) < UNDERFLOW_EXP2

    @pl.when(live())
    def _():
        @pl.when(qi > 0)
        def _():
            for g in range(1, groups):
                group_tile(g, first_diag - 1, [None] * g)

        if groups > 2:
            @pl.when(qi == 0)
            def _():
                for g in range(2, groups):
                    group_tile(g, 0, [None] * (g - 1))

        def body(state):
            n, _ = state
            for g in range(groups):
                group_tile(g, n - 2, [None, None])
            return n - 2, live()

        n, alive = lax.while_loop(lambda st: (st[0] >= 2) & st[1], body,
                                  (jnp.maximum(first_diag - 1, 0), live()))

        @pl.when((n == 1) & alive)
        def _():
            for g in range(groups):
                group_tile(g, 0, [None])

    o_ref[...] = acc_ref[...].astype(o_ref.dtype)


def _sb_attention(qkv, d_model):
    b, s, _ = qkv.shape
    dh = d_model // N_HEADS
    tq = ATTN_Q_TILE
    return pl.pallas_call(
        functools.partial(_sb_attn_kernel, blk=ATTN_BLOCK),
        out_shape=jax.ShapeDtypeStruct((b, s, d_model), BF16),
        grid=(b, N_HEADS, s // tq),
        in_specs=[pl.BlockSpec((None, tq, dh), lambda bi, h, qi: (bi, qi, h)),
                  _resident((None, s, dh), lambda bi, h, qi: (bi, 0, N_HEADS + h)),
                  _resident((None, s, dh), lambda bi, h, qi: (bi, 0, 2 * N_HEADS + h))],
        out_specs=pl.BlockSpec((None, tq, dh), lambda bi, h, qi: (bi, qi, h)),
        scratch_shapes=[pltpu.VMEM((tq, dh), F32), pltpu.VMEM((tq, 1), F32)],
        compiler_params=_params("parallel", "parallel", "arbitrary"),
        name="sb_attention",
    )(qkv, qkv, qkv)


def _row_norm_max(x):
    xf = x.astype(F32)
    return jnp.max(jnp.sqrt(jnp.sum(xf * xf, axis=-1, keepdims=True)))


def _fox_attn_kernel(cs_ref, ce_ref, kn_ref, q_ref, k_ref, v_ref, c_ref, o_ref, acc_ref, m_ref, l_ref, *, blk):
    bi, h, qi = pl.program_id(0), pl.program_id(1), pl.program_id(2)
    q = q_ref[...]
    groups = q.shape[0] // blk
    first_diag = qi * groups
    row = lax.broadcasted_iota(jnp.int32, (blk, blk), 0)
    col = lax.broadcasted_iota(jnp.int32, (blk, blk), 1)
    causal = col <= row

    qk_bound = _row_norm_max(q) * kn_ref[bi, h]
    c_first = [cs_ref[bi, h, first_diag + g] for g in range(groups)]

    def group_tile(g, first_blk, masks, new_max, first=False):
        n = len(masks)
        rows = pl.ds(g * blk, blk)
        start = pl.multiple_of(first_blk * blk, blk)
        k = k_ref[pl.ds(start, n * blk), :]
        v = v_ref[pl.ds(start, n * blk), :]
        z = _dot_nt(q_ref[rows, :], k)
        zs = []
        for i in range(n):
            zi = z[:, i * blk:(i + 1) * blk] + (c_first[g] - c_ref[pl.ds(first_blk + i, 1), :])
            if masks[i] is not None:
                zi = jnp.where(masks[i], zi, -jnp.inf)
            zs.append(zi)
        z = jnp.concatenate(zs, axis=1) if n > 1 else zs[0]
        if first:
            m_new = jnp.max(z, axis=-1, keepdims=True)
        elif new_max:
            m_old = m_ref[rows, :]
            m_new = jnp.maximum(m_old, jnp.max(z, axis=-1, keepdims=True))
        else:
            m_new = m_ref[rows, :]
        if first or new_max:
            m_ref[rows, :] = m_new
        p = jnp.exp2(z - m_new)
        lanes = sum(p[:, j * V7X_LANES:(j + 1) * V7X_LANES] for j in range(p.shape[1] // V7X_LANES))
        pv = _dot(p.astype(BF16), v)
        if first:
            l_ref[rows, :] = lanes
            acc_ref[rows, :] = pv
        elif new_max:
            alpha = jnp.exp2(m_old - m_new)
            l_ref[rows, :] = alpha * l_ref[rows, :] + lanes
            acc_ref[rows, :] = alpha * acc_ref[rows, :] + pv
        else:
            l_ref[rows, :] += lanes
            acc_ref[rows, :] += pv

    @pl.when(qi == 0)
    def _():
        group_tile(0, 0, [causal], True, first=True)
        for g in range(1, groups):
            group_tile(g, g - 1, [None, causal], True, first=True)

    @pl.when(qi > 0)
    def _():
        for g in range(groups):
            group_tile(g, first_diag + g - 1, [None, causal], True, first=True)

    def per_row(scalars):
        return jnp.concatenate([jnp.full((blk, 1), s, F32) for s in scalars], axis=0)

    c_first_rows = per_row(c_first)

    def row_slack():
        return jnp.max(c_first_rows - m_ref[...])

    def live(n, slack):
        return qk_bound + slack - ce_ref[bi, h, jnp.maximum(n - 1, 0)] > -UNDERFLOW_EXP2

    c_end_rows = per_row([ce_ref[bi, h, jnp.maximum(first_diag + g - 2, 0)] for g in range(groups)])
    fixed_m = qk_bound + jnp.max(c_first_rows - c_end_rows - m_ref[...]) < OVERFLOW_SAFE_EXP2
    slack0 = row_slack()

    def sweep(new_max):
        def slack_now():
            return row_slack() if new_max else slack0

        @pl.when(live(first_diag + groups - 2, slack0) & (qi > 0))
        def _():
            for g in range(1, groups):
                group_tile(g, first_diag - 1, [None] * g, new_max)

        if groups > 2:
            @pl.when(live(groups - 2, slack0) & (qi == 0))
            def _():
                for g in range(2, groups):
                    group_tile(g, 0, [None] * (g - 1), new_max)

        def body(state):
            n, _ = state
            for g in range(groups):
                group_tile(g, n - 2, [None, None], new_max)
            return n - 2, live(n - 2, slack_now())

        n0 = jnp.maximum(first_diag - 1, 0)
        n, alive = lax.while_loop(lambda st: (st[0] >= 2) & st[1], body, (n0, live(n0, slack_now())))

        @pl.when((n == 1) & alive)
        def _():
            for g in range(groups):
                group_tile(g, 0, [None], new_max)

    @pl.when(fixed_m)
    def _():
        sweep(False)

    @pl.when(jnp.logical_not(fixed_m))
    def _():
        sweep(True)

    l = jnp.sum(l_ref[...], axis=-1, keepdims=True)
    o_ref[...] = (acc_ref[...] / l).astype(o_ref.dtype)


def _fox_attention(q, kv, c, key_norm, d_model):
    b, s, _ = q.shape
    dh = d_model // N_HEADS
    blk = ATTN_BLOCK
    tq = ATTN_Q_TILE
    return pl.pallas_call(
        functools.partial(_fox_attn_kernel, blk=blk),
        out_shape=jax.ShapeDtypeStruct((b, s, d_model), BF16),
        grid=(b, N_HEADS, s // tq),
        in_specs=[pl.BlockSpec(memory_space=pltpu.SMEM),
                  pl.BlockSpec(memory_space=pltpu.SMEM),
                  pl.BlockSpec(memory_space=pltpu.SMEM),
                  pl.BlockSpec((None, tq, dh), lambda bi, h, qi: (bi, qi, h)),
                  _resident((None, s, dh), lambda bi, h, qi: (bi, 0, h)),
                  _resident((None, s, dh), lambda bi, h, qi: (bi, 0, N_HEADS + h)),
                  _resident((None, None, s // blk, blk), lambda bi, h, qi: (bi, h, 0, 0))],
        out_specs=pl.BlockSpec((None, tq, dh), lambda bi, h, qi: (bi, qi, h)),
        scratch_shapes=[pltpu.VMEM((tq, dh), F32), pltpu.VMEM((tq, 1), F32), pltpu.VMEM((tq, V7X_LANES), F32)],
        compiler_params=_params("parallel", "parallel", "arbitrary"),
        name="fox_attention",
    )(c[..., 0], c[..., -1], key_norm, q, kv, kv, c)


def _silu(g):
    return g / (1.0 + jnp.exp(-g))


def _ln_ple(x, ff, g, b, p, w_proj, w_gate, alpha):
    y = _layer_norm(alpha * x + ff, g, b)
    gate = 1.0 / (1.0 + jnp.exp(-_dot(y.astype(BF16), w_gate)))
    return y + _dot(p.astype(BF16), w_proj) * gate


def _mix_ln(x_ref, a_ref, wo_ref, g_ref, b_ref, alpha):
    return _layer_norm(alpha * x_ref[...] + _dot(a_ref[...], wo_ref[...]), g_ref[...], b_ref[...])


def _dense_ffn_kernel(x_ref, a_ref, wo_ref, g1_ref, b1_ref, wgu_ref, wd_ref, g_ref, b_ref, p_ref, wp_ref, wg_ref,
                      o_ref, *, alpha, d_ff):
    x = _mix_ln(x_ref, a_ref, wo_ref, g1_ref, b1_ref, alpha)
    h = _dot(x.astype(BF16), wgu_ref[...])
    act = (_silu(h[:, :d_ff]) * h[:, d_ff:]).astype(BF16)
    ff = _dot(act, wd_ref[...])
    o_ref[...] = _ln_ple(x, ff, g_ref[...], b_ref[...], p_ref[...], wp_ref[...], wg_ref[...], alpha)


def _dense_ffn(x, a, w_o, g1, b1, w_gu, w_down, g, b, p, layer, w_proj, w_gate, alpha):
    t, d = x.shape
    d_ff = w_down.shape[0]
    pd = p.shape[2]
    tm = ROW_BLOCK
    const = lambda i: (0, 0)
    return pl.pallas_call(
        functools.partial(_dense_ffn_kernel, alpha=alpha, d_ff=d_ff),
        out_shape=jax.ShapeDtypeStruct((t, d), F32),
        grid=(t // tm,),
        in_specs=[pl.BlockSpec((tm, d), lambda i: (i, 0)),
                  pl.BlockSpec((tm, d), lambda i: (i, 0)),
                  _resident((d, d), const),
                  _resident((1, d), const),
                  _resident((1, d), const),
                  _resident((d, 2 * d_ff), const),
                  _resident((d_ff, d), const),
                  _resident((1, d), const),
                  _resident((1, d), const),
                  pl.BlockSpec((None, tm, pd), lambda i: (layer, i, 0)),
                  _resident((pd, d), const),
                  _resident((d, d), const)],
        out_specs=pl.BlockSpec((tm, d), lambda i: (i, 0)),
        compiler_params=_params("parallel"),
        name="dense_ffn",
    )(x, a, w_o, g1, b1, w_gu, w_down, g, b, p, w_proj, w_gate)


def _top2_gates(logits):
    lane = lax.broadcasted_iota(jnp.int32, logits.shape, 1)
    v1 = jnp.max(logits, axis=-1, keepdims=True)
    i1 = jnp.min(jnp.where(logits == v1, lane, V7X_LANES), axis=-1, keepdims=True)
    rest = jnp.where(lane == i1, -jnp.inf, logits)
    v2 = jnp.max(rest, axis=-1, keepdims=True)
    i2 = jnp.min(jnp.where(rest == v2, lane, V7X_LANES), axis=-1, keepdims=True)
    e2 = jnp.exp(v2 - v1)
    g1 = 1.0 / (1.0 + e2)
    g2 = e2 / (1.0 + e2)
    return i1, g1, i2, g2


def _moe_ffn_kernel(x_ref, a_ref, wo_ref, g1_ref, b1_ref, wr_ref, wg_ref, wu_ref, wd_ref, g_ref, b_ref, p_ref,
                    wp_ref, wgate_ref, o_ref, acc_ref, route_ref, x1_ref, xb_ref, *, alpha, f_exp):
    pr = pl.program_id(1)
    tm = x_ref.shape[0]

    @pl.when(pr == 0)
    def _():
        x = _mix_ln(x_ref, a_ref, wo_ref, g1_ref, b1_ref, alpha)
        x1_ref[...] = x
        xb_ref[...] = x.astype(BF16)
        lane = lax.broadcasted_iota(jnp.int32, (tm, V7X_LANES), 1)
        logits = jnp.where(lane < N_EXPERTS, _hilo_dot(x, wr_ref[...]), -jnp.inf)
        i1, g1, i2, g2 = _top2_gates(logits)
        route_ref[...] = jnp.where(lane == i1, g1, 0.0) + jnp.where(lane == i2, g2, 0.0)
        acc_ref[...] = jnp.zeros_like(acc_ref)

    xb = xb_ref[...]
    act = _silu(_dot(xb, wg_ref[...])) * _dot(xb, wu_ref[...])
    lane = lax.broadcasted_iota(jnp.int32, (tm, V7X_LANES), 1)
    gates = route_ref[...]
    gate_a = jnp.sum(jnp.where(lane == 2 * pr, gates, 0.0), axis=-1, keepdims=True)
    gate_b = jnp.sum(jnp.where(lane == 2 * pr + 1, gates, 0.0), axis=-1, keepdims=True)
    lo = (f_exp // V7X_LANES) * V7X_LANES
    hi = lo + V7X_LANES
    mixed = jnp.where(lane < f_exp - lo, gate_a, gate_b)
    act = jnp.concatenate([act[:, :lo] * gate_a, act[:, lo:hi] * mixed, act[:, hi:] * gate_b], axis=1)
    acc_ref[...] += _dot(act.astype(BF16), wd_ref[...])

    @pl.when(pr == pl.num_programs(1) - 1)
    def _():
        o_ref[...] = _ln_ple(x1_ref[...], acc_ref[...], g_ref[...], b_ref[...], p_ref[...], wp_ref[...],
                             wgate_ref[...], alpha)


def _moe_ffn(x, a, w_o, g1, b1, w_router2, w_g, w_u, w_d, g, b, p, layer, w_proj, w_gate, alpha):
    t, d = x.shape
    f_exp = w_d.shape[0] // N_EXPERTS
    pd = p.shape[2]
    tm = ROW_BLOCK
    const = lambda i, pr: (0, 0)
    return pl.pallas_call(
        functools.partial(_moe_ffn_kernel, alpha=alpha, f_exp=f_exp),
        out_shape=jax.ShapeDtypeStruct((t, d), F32),
        grid=(t // tm, N_EXPERTS // 2),
        in_specs=[pl.BlockSpec((tm, d), lambda i, pr: (i, 0)),
                  pl.BlockSpec((tm, d), lambda i, pr: (i, 0)),
                  _resident((d, d), const),
                  _resident((1, d), const),
                  _resident((1, d), const),
                  _resident((d, V7X_LANES), const),
                  pl.BlockSpec((d, 2 * f_exp), lambda i, pr: (0, pr)),
                  pl.BlockSpec((d, 2 * f_exp), lambda i, pr: (0, pr)),
                  pl.BlockSpec((2 * f_exp, d), lambda i, pr: (pr, 0)),
                  _resident((1, d), const),
                  _resident((1, d), const),
                  pl.BlockSpec((None, tm, pd), lambda i, pr: (layer, i, 0)),
                  _resident((pd, d), const),
                  _resident((d, d), const)],
        out_specs=pl.BlockSpec((tm, d), lambda i, pr: (i, 0)),
        scratch_shapes=[pltpu.VMEM((tm, d), F32), pltpu.VMEM((tm, V7X_LANES), F32),
                        pltpu.VMEM((tm, d), F32), pltpu.VMEM((tm, d), BF16)],
        compiler_params=_params("parallel", "arbitrary"),
        name="moe_ffn",
    )(x, a, w_o, g1, b1, w_router2, w_g, w_u, w_d, g, b, p, w_proj, w_gate)


def kernel(x, p, a_w_qkv, a_w_o, kvf_w, kvf_b_f, b_w_q, b_w_o, ffn_w_gu, ffn_w_down, moe_w_router, moe_w_gu,
           moe_w_down, ple_w_proj, ple_w_gate, ln_g, ln_b):
    bsz, seq, d = x.shape
    depth = p.shape[0]
    n_a = a_w_qkv.shape[0]
    t = bsz * seq
    dh = d // N_HEADS
    alpha = (2.0 * depth) ** 0.25
    scale = LOG2E / math.sqrt(dh)
    f_exp = moe_w_down.shape[2]

    xf = x.reshape(t, d)
    pf = p.reshape(depth, t, p.shape[-1])

    c_sh = kv_sh = kn_sh = None
    for i in range(depth):
        g1, b1 = ln_g[i, 0][None, :], ln_b[i, 0][None, :]
        g2, b2 = ln_g[i, 1][None, :], ln_b[i, 1][None, :]
        if i < n_a:
            w_qkv = jnp.concatenate([a_w_qkv[i][:, :d] * scale, a_w_qkv[i][:, d:]], axis=1).astype(BF16)
            qkv = _proj(xf, w_qkv).reshape(bsz, seq, 3 * d)
            mix = _sb_attention(qkv, d).reshape(t, d)
            w_o = a_w_o[i].astype(BF16)
        else:
            if i == n_a:
                kv, f_logit, kn2 = _proj_kvf(xf, kvf_w[:, :2 * d].astype(BF16), _hilo_weight(kvf_w[:, 2 * d:]).T)
                kv_sh = kv.reshape(bsz, seq, 2 * d)
                kn_sh = 1.02 * jnp.sqrt(jnp.max(kn2[:, 0, :N_HEADS].reshape(bsz, -1, N_HEADS), axis=1))
                c_sh = _log_forget_cumsum(f_logit, kvf_b_f, bsz, seq)
                c_sh = c_sh.reshape(bsz, N_HEADS, seq // ATTN_BLOCK, ATTN_BLOCK)
            j = i - n_a
            q = _proj(xf, (b_w_q[j] * scale).astype(BF16)).reshape(bsz, seq, d)
            mix = _fox_attention(q, kv_sh, c_sh, kn_sh, d).reshape(t, d)
            w_o = b_w_o[j].astype(BF16)

        w_proj = ple_w_proj[i].astype(BF16)
        w_gate = ple_w_gate[i].astype(BF16)
        if i % 2 == 0:
            xf = _dense_ffn(xf, mix, w_o, g1, b1, ffn_w_gu[i // 2].astype(BF16), ffn_w_down[i // 2].astype(BF16),
                            g2, b2, pf, i, w_proj, w_gate, alpha)
        else:
            m = i // 2
            w_gu = moe_w_gu[m].astype(BF16)
            w_g = w_gu[:, :, :f_exp].transpose(1, 0, 2).reshape(d, N_EXPERTS * f_exp)
            w_u = w_gu[:, :, f_exp:].transpose(1, 0, 2).reshape(d, N_EXPERTS * f_exp)
            w_d = moe_w_down[m].reshape(N_EXPERTS * f_exp, d).astype(BF16)
            xf = _moe_ffn(xf, mix, w_o, g1, b1, _hilo_weight(moe_w_router[m]), w_g, w_u, w_d, g2, b2, pf, i,
                          w_proj, w_gate, alpha)
    return xf.reshape(bsz, seq, d)
```

```python
import functools
import math

import jax
import jax.numpy as jnp
from jax import lax
from jax.experimental import pallas as pl
from jax.experimental.pallas import tpu as pltpu

F32 = jnp.float32
BF16 = jnp.bfloat16

N_HEADS = 4
N_EXPERTS = 8
LN_EPS = 1e-5

V7X_LANES = 128
V7X_VMEM_LIMIT = 56 * 1024 * 1024

LOG2E = 1.4426950408889634
UNDERFLOW_EXP2 = 160.0
OVERFLOW_SAFE_EXP2 = 64.0

ROW_BLOCK = 512
ATTN_BLOCK = 256
FOX_Q_TILE = 1024
SB_Q_TILE = 2048


def _params(*sem):
    return pltpu.CompilerParams(dimension_semantics=sem, vmem_limit_bytes=V7X_VMEM_LIMIT)


def _resident(shape, index_map):
    return pl.BlockSpec(shape, index_map, pipeline_mode=pl.Buffered(1))


def _dot(a, b):
    return jnp.dot(a, b, preferred_element_type=F32)


def _dot_nt(a, b):
    return lax.dot_general(a, b, (((1,), (1,)), ((), ())), preferred_element_type=F32)


def _split2(x):
    hi = x.astype(BF16)
    lo = (x - hi.astype(F32)).astype(BF16)
    return hi, lo


def _split3(x):
    a = x.astype(BF16)
    r = x - a.astype(F32)
    b = r.astype(BF16)
    c = (r - b.astype(F32)).astype(BF16)
    return a, b, c


def _hilo_weight(w):
    k, n = w.shape
    hi = w.astype(BF16)
    lo = (w - hi.astype(F32)).astype(BF16)
    out = jnp.zeros((k, V7X_LANES), BF16)
    out = out.at[:, :n].set(hi)
    out = out.at[:, 64:64 + n].set(lo)
    return out


def _hilo_dot(x, w2):
    xh, xl = _split2(x)
    a = _dot(xh, w2)
    b = _dot(xl, w2)
    return a + pltpu.roll(a, 64, 1) + b


def _layer_norm(y, g, b):
    mu = jnp.mean(y, axis=-1, keepdims=True)
    d = y - mu
    var = jnp.mean(d * d, axis=-1, keepdims=True)
    return d * lax.rsqrt(var + LN_EPS) * g + b


def _proj_kernel(x_ref, w_ref, o_ref):
    o_ref[...] = _dot(x_ref[...].astype(BF16), w_ref[...]).astype(o_ref.dtype)


def _proj(x, w):
    t, k = x.shape
    n = w.shape[1]
    return pl.pallas_call(
        _proj_kernel,
        out_shape=jax.ShapeDtypeStruct((t, n), BF16),
        grid=(t // ROW_BLOCK,),
        in_specs=[pl.BlockSpec((ROW_BLOCK, k), lambda i: (i, 0)),
                  _resident((k, n), lambda i: (0, 0))],
        out_specs=pl.BlockSpec((ROW_BLOCK, n), lambda i: (i, 0)),
        compiler_params=_params("parallel"),
        name="proj",
    )(x, w)


def _proj_kvf_kernel(x_ref, w_ref, wf_ref, o_ref, f_ref, kn_ref, *, dh):
    x = x_ref[...]
    kv = _dot(x.astype(BF16), w_ref[...])
    o_ref[...] = kv.astype(o_ref.dtype)
    xh, xl = _split2(x)
    a = _dot_nt(wf_ref[...], xh)
    b = _dot_nt(wf_ref[...], xl)
    f_ref[...] = a[0:8, :] + a[64:72, :] + b[0:8, :]
    lane = lax.broadcasted_iota(jnp.int32, kn_ref.shape, 1)
    best = jnp.zeros(kn_ref.shape, F32)
    for h in range(N_HEADS):
        kh = kv[:, h * dh:(h + 1) * dh]
        best = jnp.where(lane == h, jnp.max(jnp.sum(kh * kh, axis=-1, keepdims=True)), best)
    kn_ref[...] = best


def _proj_kvf(x, w, wf2t):
    t, k = x.shape
    n = w.shape[1]
    nblk = t // ROW_BLOCK
    return pl.pallas_call(
        functools.partial(_proj_kvf_kernel, dh=n // (2 * N_HEADS)),
        out_shape=(jax.ShapeDtypeStruct((t, n), BF16),
                   jax.ShapeDtypeStruct((8, t), F32),
                   jax.ShapeDtypeStruct((nblk, 8, V7X_LANES), F32)),
        grid=(nblk,),
        in_specs=[pl.BlockSpec((ROW_BLOCK, k), lambda i: (i, 0)),
                  _resident((k, n), lambda i: (0, 0)),
                  _resident((V7X_LANES, k), lambda i: (0, 0))],
        out_specs=(pl.BlockSpec((ROW_BLOCK, n), lambda i: (i, 0)),
                   pl.BlockSpec((8, ROW_BLOCK), lambda i: (0, i)),
                   pl.BlockSpec((None, 8, V7X_LANES), lambda i: (i, 0, 0))),
        compiler_params=_params("parallel"),
        name="proj_kvf",
    )(x, w, wf2t)


def _cumsum_kernel(v_ref, bias_ref, c_ref, *, rows_per_seq):
    z = v_ref[...] + bias_ref[...]
    a = (jnp.minimum(z, 0.0) - jnp.log(1.0 + jnp.exp(-jnp.abs(z)))) * LOG2E
    r = a.shape[0]
    j = lax.broadcasted_iota(jnp.int32, (V7X_LANES, V7X_LANES), 0)
    s = lax.broadcasted_iota(jnp.int32, (V7X_LANES, V7X_LANES), 1)
    upper = jnp.where(j <= s, 1.0, 0.0).astype(BF16)
    a1, a2, a3 = _split3(a)
    within = _dot(a1, upper) + _dot(a2, upper) + _dot(a3, upper)
    ri = lax.broadcasted_iota(jnp.int32, (r, r), 0)
    rj = lax.broadcasted_iota(jnp.int32, (r, r), 1)
    same = (ri // rows_per_seq) == (rj // rows_per_seq)
    before = jnp.where(same & (rj < ri), 1.0, 0.0).astype(BF16)
    w1, w2, w3 = _split3(within)
    offs = _dot(before, w1) + _dot(before, w2) + _dot(before, w3)
    c_ref[...] = within + offs[:, V7X_LANES - 1:V7X_LANES]


def _log_forget_cumsum(f_logit_t, bias_f, bsz, seq):
    rows_per_seq = seq // V7X_LANES
    fl = f_logit_t[:N_HEADS].reshape(N_HEADS * bsz * rows_per_seq, V7X_LANES)
    bias = jnp.broadcast_to(bias_f.astype(F32)[:, None, None, None],
                            (N_HEADS, bsz, rows_per_seq, 1)).reshape(-1, 1)
    c = pl.pallas_call(
        functools.partial(_cumsum_kernel, rows_per_seq=rows_per_seq),
        out_shape=jax.ShapeDtypeStruct(fl.shape, F32),
        compiler_params=_params(),
        name="forget_cumsum",
    )(fl, bias)
    return c.reshape(N_HEADS, bsz, seq).transpose(1, 0, 2)


def _softplus2(z):
    return jnp.maximum(z, 0.0) + jnp.log(1.0 + jnp.exp2(-jnp.abs(z))) * LOG2E


def _sb_attn_kernel(q_ref, k_ref, v_ref, o_ref, acc_ref, carry_ref, *, blk):
    qi = pl.program_id(2)
    groups = q_ref.shape[0] // blk
    first_diag = qi * groups
    row = lax.broadcasted_iota(jnp.int32, (blk, blk), 0)
    col = lax.broadcasted_iota(jnp.int32, (blk, blk), 1)
    causal = col < row
    suffix = jnp.where(row >= col, 1.0, 0.0).astype(BF16)

    def group_tile(g, first_blk, masks, first=False):
        n = len(masks)
        rows = pl.ds(g * blk, blk)
        start = pl.multiple_of(first_blk * blk, blk)
        k = k_ref[pl.ds(start, n * blk), :]
        v = v_ref[pl.ds(start, n * blk), :]
        z = _dot_nt(q_ref[rows, :], k)
        later = None if first else carry_ref[rows, :]
        ps = [None] * n
        for i in reversed(range(n)):
            zi = z[:, i * blk:(i + 1) * blk]
            if masks[i] is not None:
                zi = jnp.where(masks[i], zi, -jnp.inf)
            sp = _softplus2(zi)
            r = _dot(sp.astype(BF16), suffix)
            if later is not None:
                r = r + later
            ps[i] = jnp.exp2(zi - r).astype(BF16)
            later = r[:, 0:1]
        pv = _dot(jnp.concatenate(ps, axis=1) if n > 1 else ps[0], v)
        if first:
            acc_ref[rows, :] = pv
        else:
            acc_ref[rows, :] += pv
        carry_ref[rows, :] = later

    @pl.when(qi == 0)
    def _():
        group_tile(0, 0, [causal], first=True)
        for g in range(1, groups):
            group_tile(g, g - 1, [None, causal], first=True)

    @pl.when(qi > 0)
    def _():
        for g in range(groups):
            group_tile(g, first_diag + g - 1, [None, causal], first=True)

    def live():
        return jnp.min(carry_ref[...]) < UNDERFLOW_EXP2

    @pl.when(live())
    def _():
        @pl.when(qi > 0)
        def _():
            for g in range(1, groups):
                group_tile(g, first_diag - 1, [None] * g)

        if groups > 2:
            @pl.when(qi == 0)
            def _():
                for g in range(2, groups):
                    group_tile(g, 0, [None] * (g - 1))

        def body(state):
            n, _ = state
            for g in range(groups):
                group_tile(g, n - 2, [None, None])
            return n - 2, live()

        n, alive = lax.while_loop(lambda st: (st[0] >= 2) & st[1], body,
                                  (jnp.maximum(first_diag - 1, 0), live()))

        @pl.when((n == 1) & alive)
        def _():
            for g in range(groups):
                group_tile(g, 0, [None])

    o_ref[...] = acc_ref[...].astype(o_ref.dtype)


def _sb_attention(qkv, d_model):
    b, s, _ = qkv.shape
    dh = d_model // N_HEADS
    tq = SB_Q_TILE
    return pl.pallas_call(
        functools.partial(_sb_attn_kernel, blk=ATTN_BLOCK),
        out_shape=jax.ShapeDtypeStruct((b, s, d_model), BF16),
        grid=(b, N_HEADS, s // tq),
        in_specs=[pl.BlockSpec((None, tq, dh), lambda bi, h, qi: (bi, qi, h)),
                  _resident((None, s, dh), lambda bi, h, qi: (bi, 0, N_HEADS + h)),
                  _resident((None, s, dh), lambda bi, h, qi: (bi, 0, 2 * N_HEADS + h))],
        out_specs=pl.BlockSpec((None, tq, dh), lambda bi, h, qi: (bi, qi, h)),
        scratch_shapes=[pltpu.VMEM((tq, dh), F32), pltpu.VMEM((tq, 1), F32)],
        compiler_params=_params("parallel", "parallel", "arbitrary"),
        name="sb_attention",
    )(qkv, qkv, qkv)


def _row_norm_max(x):
    xf = x.astype(F32)
    return jnp.max(jnp.sqrt(jnp.sum(xf * xf, axis=-1, keepdims=True)))


def _fox_attn_kernel(cs_ref, ce_ref, kn_ref, q_ref, k_ref, v_ref, c_ref, o_ref, acc_ref, m_ref, l_ref, *, blk):
    bi, h, qi = pl.program_id(0), pl.program_id(1), pl.program_id(2)
    q = q_ref[...]
    groups = q.shape[0] // blk
    first_diag = qi * groups
    row = lax.broadcasted_iota(jnp.int32, (blk, blk), 0)
    col = lax.broadcasted_iota(jnp.int32, (blk, blk), 1)
    causal = col <= row

    qk_bound = _row_norm_max(q) * kn_ref[bi, h]
    c_first = [cs_ref[bi, h, first_diag + g] for g in range(groups)]

    def group_tile(g, first_blk, masks, new_max, first=False):
        n = len(masks)
        rows = pl.ds(g * blk, blk)
        start = pl.multiple_of(first_blk * blk, blk)
        k = k_ref[pl.ds(start, n * blk), :]
        v = v_ref[pl.ds(start, n * blk), :]
        z = _dot_nt(q_ref[rows, :], k)
        zs = []
        for i in range(n):
            zi = z[:, i * blk:(i + 1) * blk] + (c_first[g] - c_ref[pl.ds(first_blk + i, 1), :])
            if masks[i] is not None:
                zi = jnp.where(masks[i], zi, -jnp.inf)
            zs.append(zi)
        z = jnp.concatenate(zs, axis=1) if n > 1 else zs[0]
        if first:
            m_new = jnp.max(z, axis=-1, keepdims=True)
        elif new_max:
            m_old = m_ref[rows, :]
            m_new = jnp.maximum(m_old, jnp.max(z, axis=-1, keepdims=True))
        else:
            m_new = m_ref[rows, :]
        if first or new_max:
            m_ref[rows, :] = m_new
        p = jnp.exp2(z - m_new)
        lanes = sum(p[:, j * V7X_LANES:(j + 1) * V7X_LANES] for j in range(p.shape[1] // V7X_LANES))
        pv = _dot(p.astype(BF16), v)
        if first:
            l_ref[rows, :] = lanes
            acc_ref[rows, :] = pv
        elif new_max:
            alpha = jnp.exp2(m_old - m_new)
            l_ref[rows, :] = alpha * l_ref[rows, :] + lanes
            acc_ref[rows, :] = alpha * acc_ref[rows, :] + pv
        else:
            l_ref[rows, :] += lanes
            acc_ref[rows, :] += pv

    @pl.when(qi == 0)
    def _():
        group_tile(0, 0, [causal], True, first=True)
        for g in range(1, groups):
            group_tile(g, g - 1, [None, causal], True, first=True)

    @pl.when(qi > 0)
    def _():
        for g in range(groups):
            group_tile(g, first_diag + g - 1, [None, causal], True, first=True)

    def per_row(scalars):
        return jnp.concatenate([jnp.full((blk, 1), s, F32) for s in scalars], axis=0)

    c_first_rows = per_row(c_first)

    def row_slack():
        return jnp.max(c_first_rows - m_ref[...])

    def live(n, slack):
        return qk_bound + slack - ce_ref[bi, h, jnp.maximum(n - 1, 0)] > -UNDERFLOW_EXP2

    c_end_rows = per_row([ce_ref[bi, h, jnp.maximum(first_diag + g - 2, 0)] for g in range(groups)])
    fixed_m = qk_bound + jnp.max(c_first_rows - c_end_rows - m_ref[...]) < OVERFLOW_SAFE_EXP2
    slack0 = row_slack()

    def sweep(new_max):
        def slack_now():
            return row_slack() if new_max else slack0

        @pl.when(live(first_diag + groups - 2, slack0) & (qi > 0))
        def _():
            for g in range(1, groups):
                group_tile(g, first_diag - 1, [None] * g, new_max)

        if groups > 2:
            @pl.when(live(groups - 2, slack0) & (qi == 0))
            def _():
                for g in range(2, groups):
                    group_tile(g, 0, [None] * (g - 1), new_max)

        def body(state):
            n, _ = state
            for g in range(groups):
                group_tile(g, n - 2, [None, None], new_max)
            return n - 2, live(n - 2, slack_now())

        n0 = jnp.maximum(first_diag - 1, 0)
        n, alive = lax.while_loop(lambda st: (st[0] >= 2) & st[1], body, (n0, live(n0, slack_now())))

        @pl.when((n == 1) & alive)
        def _():
            for g in range(groups):
                group_tile(g, 0, [None], new_max)

    @pl.when(fixed_m)
    def _():
        sweep(False)

    @pl.when(jnp.logical_not(fixed_m))
    def _():
        sweep(True)

    l = jnp.sum(l_ref[...], axis=-1, keepdims=True)
    o_ref[...] = (acc_ref[...] / l).astype(o_ref.dtype)


def _fox_attention(q, kv, c, key_norm, d_model):
    b, s, _ = q.shape
    dh = d_model // N_HEADS
    blk = ATTN_BLOCK
    tq = FOX_Q_TILE
    return pl.pallas_call(
        functools.partial(_fox_attn_kernel, blk=blk),
        out_shape=jax.ShapeDtypeStruct((b, s, d_model), BF16),
        grid=(b, N_HEADS, s // tq),
        in_specs=[pl.BlockSpec(memory_space=pltpu.SMEM),
                  pl.BlockSpec(memory_space=pltpu.SMEM),
                  pl.BlockSpec(memory_space=pltpu.SMEM),
                  pl.BlockSpec((None, tq, dh), lambda bi, h, qi: (bi, qi, h)),
                  _resident((None, s, dh), lambda bi, h, qi: (bi, 0, h)),
                  _resident((None, s, dh), lambda bi, h, qi: (bi, 0, N_HEADS + h)),
                  _resident((None, None, s // blk, blk), lambda bi, h, qi: (bi, h, 0, 0))],
        out_specs=pl.BlockSpec((None, tq, dh), lambda bi, h, qi: (bi, qi, h)),
        scratch_shapes=[pltpu.VMEM((tq, dh), F32), pltpu.VMEM((tq, 1), F32), pltpu.VMEM((tq, V7X_LANES), F32)],
        compiler_params=_params("parallel", "parallel", "arbitrary"),
        name="fox_attention",
    )(c[..., 0], c[..., -1], key_norm, q, kv, kv, c)


def _silu(g):
    return g / (1.0 + jnp.exp(-g))


def _ln_ple(x, ff, g, b, p, w_proj, w_gate, alpha):
    y = _layer_norm(alpha * x + ff, g, b)
    gate = 1.0 / (1.0 + jnp.exp(-_dot(y.astype(BF16), w_gate)))
    return y + _dot(p.astype(BF16), w_proj) * gate


def _mix_ln(x_ref, a_ref, wo_ref, g_ref, b_ref, alpha):
    return _layer_norm(alpha * x_ref[...] + _dot(a_ref[...], wo_ref[...]), g_ref[...], b_ref[...])


def _dense_ffn_kernel(x_ref, a_ref, wo_ref, g1_ref, b1_ref, wgu_ref, wd_ref, g_ref, b_ref, p_ref, wp_ref, wg_ref,
                      o_ref, *, alpha, d_ff):
    x = _mix_ln(x_ref, a_ref, wo_ref, g1_ref, b1_ref, alpha)
    h = _dot(x.astype(BF16), wgu_ref[...])
    act = (_silu(h[:, :d_ff]) * h[:, d_ff:]).astype(BF16)
    ff = _dot(act, wd_ref[...])
    o_ref[...] = _ln_ple(x, ff, g_ref[...], b_ref[...], p_ref[...], wp_ref[...], wg_ref[...], alpha)


def _dense_ffn(x, a, w_o, g1, b1, w_gu, w_down, g, b, p, layer, w_proj, w_gate, alpha):
    t, d = x.shape
    d_ff = w_down.shape[0]
    pd = p.shape[2]
    tm = ROW_BLOCK
    const = lambda i: (0, 0)
    return pl.pallas_call(
        functools.partial(_dense_ffn_kernel, alpha=alpha, d_ff=d_ff),
        out_shape=jax.ShapeDtypeStruct((t, d), F32),
        grid=(t // tm,),
        in_specs=[pl.BlockSpec((tm, d), lambda i: (i, 0)),
                  pl.BlockSpec((tm, d), lambda i: (i, 0)),
                  _resident((d, d), const),
                  _resident((1, d), const),
                  _resident((1, d), const),
                  _resident((d, 2 * d_ff), const),
                  _resident((d_ff, d), const),
                  _resident((1, d), const),
                  _resident((1, d), const),
                  pl.BlockSpec((None, tm, pd), lambda i: (layer, i, 0)),
                  _resident((pd, d), const),
                  _resident((d, d), const)],
        out_specs=pl.BlockSpec((tm, d), lambda i: (i, 0)),
        compiler_params=_params("parallel"),
        name="dense_ffn",
    )(x, a, w_o, g1, b1, w_gu, w_down, g, b, p, w_proj, w_gate)


def _top2_gates(logits):
    lane = lax.broadcasted_iota(jnp.int32, logits.shape, 1)
    v1 = jnp.max(logits, axis=-1, keepdims=True)
    i1 = jnp.min(jnp.where(logits == v1, lane, V7X_LANES), axis=-1, keepdims=True)
    rest = jnp.where(lane == i1, -jnp.inf, logits)
    v2 = jnp.max(rest, axis=-1, keepdims=True)
    i2 = jnp.min(jnp.where(rest == v2, lane, V7X_LANES), axis=-1, keepdims=True)
    e2 = jnp.exp(v2 - v1)
    g1 = 1.0 / (1.0 + e2)
    g2 = e2 / (1.0 + e2)
    return i1, g1, i2, g2


def _moe_ffn_kernel(x_ref, a_ref, wo_ref, g1_ref, b1_ref, wr_ref, wg_ref, wu_ref, wd_ref, g_ref, b_ref, p_ref,
                    wp_ref, wgate_ref, o_ref, acc_ref, route_ref, x1_ref, xb_ref, *, alpha, f_exp):
    pr = pl.program_id(1)
    tm = x_ref.shape[0]

    @pl.when(pr == 0)
    def _():
        x = _mix_ln(x_ref, a_ref, wo_ref, g1_ref, b1_ref, alpha)
        x1_ref[...] = x
        xb_ref[...] = x.astype(BF16)
        lane = lax.broadcasted_iota(jnp.int32, (tm, V7X_LANES), 1)
        logits = jnp.where(lane < N_EXPERTS, _hilo_dot(x, wr_ref[...]), -jnp.inf)
        i1, g1, i2, g2 = _top2_gates(logits)
        route_ref[...] = jnp.where(lane == i1, g1, 0.0) + jnp.where(lane == i2, g2, 0.0)
        acc_ref[...] = jnp.zeros_like(acc_ref)

    xb = xb_ref[...]
    act = _silu(_dot(xb, wg_ref[...])) * _dot(xb, wu_ref[...])
    lane = lax.broadcasted_iota(jnp.int32, (tm, V7X_LANES), 1)
    gates = route_ref[...]
    gate_a = jnp.sum(jnp.where(lane == 2 * pr, gates, 0.0), axis=-1, keepdims=True)
    gate_b = jnp.sum(jnp.where(lane == 2 * pr + 1, gates, 0.0), axis=-1, keepdims=True)
    lo = (f_exp // V7X_LANES) * V7X_LANES
    hi = lo + V7X_LANES
    mixed = jnp.where(lane < f_exp - lo, gate_a, gate_b)
    act = jnp.concatenate([act[:, :lo] * gate_a, act[:, lo:hi] * mixed, act[:, hi:] * gate_b], axis=1)
    acc_ref[...] += _dot(act.astype(BF16), wd_ref[...])

    @pl.when(pr == pl.num_programs(1) - 1)
    def _():
        o_ref[...] = _ln_ple(x1_ref[...], acc_ref[...], g_ref[...], b_ref[...], p_ref[...], wp_ref[...],
                             wgate_ref[...], alpha)


def _moe_ffn(x, a, w_o, g1, b1, w_router2, w_g, w_u, w_d, g, b, p, layer, w_proj, w_gate, alpha):
    t, d = x.shape
    f_exp = w_d.shape[0] // N_EXPERTS
    pd = p.shape[2]
    tm = ROW_BLOCK
    const = lambda i, pr: (0, 0)
    return pl.pallas_call(
        functools.partial(_moe_ffn_kernel, alpha=alpha, f_exp=f_exp),
        out_shape=jax.ShapeDtypeStruct((t, d), F32),
        grid=(t // tm, N_EXPERTS // 2),
        in_specs=[pl.BlockSpec((tm, d), lambda i, pr: (i, 0)),
                  pl.BlockSpec((tm, d), lambda i, pr: (i, 0)),
                  _resident((d, d), const),
                  _resident((1, d), const),
                  _resident((1, d), const),
                  _resident((d, V7X_LANES), const),
                  pl.BlockSpec((d, 2 * f_exp), lambda i, pr: (0, pr)),
                  pl.BlockSpec((d, 2 * f_exp), lambda i, pr: (0, pr)),
                  pl.BlockSpec((2 * f_exp, d), lambda i, pr: (pr, 0)),
                  _resident((1, d), const),
                  _resident((1, d), const),
                  pl.BlockSpec((None, tm, pd), lambda i, pr: (layer, i, 0)),
                  _resident((pd, d), const),
                  _resident((d, d), const)],
        out_specs=pl.BlockSpec((tm, d), lambda i, pr: (i, 0)),
        scratch_shapes=[pltpu.VMEM((tm, d), F32), pltpu.VMEM((tm, V7X_LANES), F32),
                        pltpu.VMEM((tm, d), F32), pltpu.VMEM((tm, d), BF16)],
        compiler_params=_params("parallel", "arbitrary"),
        name="moe_ffn",
    )(x, a, w_o, g1, b1, w_router2, w_g, w_u, w_d, g, b, p, w_proj, w_gate)


def kernel(x, p, a_w_qkv, a_w_o, kvf_w, kvf_b_f, b_w_q, b_w_o, ffn_w_gu, ffn_w_down, moe_w_router, moe_w_gu,
           moe_w_down, ple_w_proj, ple_w_gate, ln_g, ln_b):
    bsz, seq, d = x.shape
    depth = p.shape[0]
    n_a = a_w_qkv.shape[0]
    t = bsz * seq
    dh = d // N_HEADS
    alpha = (2.0 * depth) ** 0.25
    scale = LOG2E / math.sqrt(dh)
    f_exp = moe_w_down.shape[2]

    xf = x.reshape(t, d)
    pf = p.reshape(depth, t, p.shape[-1])

    c_sh = kv_sh = kn_sh = None
    for i in range(depth):
        g1, b1 = ln_g[i, 0][None, :], ln_b[i, 0][None, :]
        g2, b2 = ln_g[i, 1][None, :], ln_b[i, 1][None, :]
        if i < n_a:
            w_qkv = jnp.concatenate([a_w_qkv[i][:, :d] * scale, a_w_qkv[i][:, d:]], axis=1).astype(BF16)
            qkv = _proj(xf, w_qkv).reshape(bsz, seq, 3 * d)
            mix = _sb_attention(qkv, d).reshape(t, d)
            w_o = a_w_o[i].astype(BF16)
        else:
            if i == n_a:
                kv, f_logit, kn2 = _proj_kvf(xf, kvf_w[:, :2 * d].astype(BF16), _hilo_weight(kvf_w[:, 2 * d:]).T)
                kv_sh = kv.reshape(bsz, seq, 2 * d)
                kn_sh = 1.02 * jnp.sqrt(jnp.max(kn2[:, 0, :N_HEADS].reshape(bsz, -1, N_HEADS), axis=1))
                c_sh = _log_forget_cumsum(f_logit, kvf_b_f, bsz, seq)
                c_sh = c_sh.reshape(bsz, N_HEADS, seq // ATTN_BLOCK, ATTN_BLOCK)
            j = i - n_a
            q = _proj(xf, (b_w_q[j] * scale).astype(BF16)).reshape(bsz, seq, d)
            mix = _fox_attention(q, kv_sh, c_sh, kn_sh, d).reshape(t, d)
            w_o = b_w_o[j].astype(BF16)

        w_proj = ple_w_proj[i].astype(BF16)
        w_gate = ple_w_gate[i].astype(BF16)
        if i % 2 == 0:
            xf = _dense_ffn(xf, mix, w_o, g1, b1, ffn_w_gu[i // 2].astype(BF16), ffn_w_down[i // 2].astype(BF16),
                            g2, b2, pf, i, w_proj, w_gate, alpha)
        else:
            m = i // 2
            w_gu = moe_w_gu[m].astype(BF16)
            w_g = w_gu[:, :, :f_exp].transpose(1, 0, 2).reshape(d, N_EXPERTS * f_exp)
            w_u = w_gu[:, :, f_exp:].transpose(1, 0, 2).reshape(d, N_EXPERTS * f_exp)
            w_d = moe_w_down[m].reshape(N_EXPERTS * f_exp, d).astype(BF16)
            xf = _moe_ffn(xf, mix, w_o, g1, b1, _hilo_weight(moe_w_router[m]), w_g, w_u, w_d, g2, b2, pf, i,
                          w_proj, w_gate, alpha)
    return xf.reshape(bsz, seq, d)
```

```python
import functools
import math

import jax
import jax.numpy as jnp
from jax import lax
from jax.experimental import pallas as pl
from jax.experimental.pallas import tpu as pltpu

F32 = jnp.float32
BF16 = jnp.bfloat16

N_HEADS = 4
N_EXPERTS = 8
LN_EPS = 1e-5

V7X_LANES = 128
V7X_VMEM_LIMIT = 56 * 1024 * 1024

LOG2E = 1.4426950408889634
UNDERFLOW_EXP2 = 160.0
OVERFLOW_SAFE_EXP2 = 64.0

ROW_BLOCK = 512
ATTN_BLOCK = 256
ATTN_Q_TILE = 1024


def _params(*sem):
    return pltpu.CompilerParams(dimension_semantics=sem, vmem_limit_bytes=V7X_VMEM_LIMIT)


def _resident(shape, index_map):
    return pl.BlockSpec(shape, index_map, pipeline_mode=pl.Buffered(1))


def _dot(a, b):
    return jnp.dot(a, b, preferred_element_type=F32)


def _dot_nt(a, b):
    return lax.dot_general(a, b, (((1,), (1,)), ((), ())), preferred_element_type=F32)


def _split2(x):
    hi = x.astype(BF16)
    lo = (x - hi.astype(F32)).astype(BF16)
    return hi, lo


def _split3(x):
    a = x.astype(BF16)
    r = x - a.astype(F32)
    b = r.astype(BF16)
    c = (r - b.astype(F32)).astype(BF16)
    return a, b, c


def _hilo_weight(w):
    k, n = w.shape
    hi = w.astype(BF16)
    lo = (w - hi.astype(F32)).astype(BF16)
    out = jnp.zeros((k, V7X_LANES), BF16)
    out = out.at[:, :n].set(hi)
    out = out.at[:, 64:64 + n].set(lo)
    return out


def _hilo_dot(x, w2):
    xh, xl = _split2(x)
    a = _dot(xh, w2)
    b = _dot(xl, w2)
    return a + pltpu.roll(a, 64, 1) + b


def _layer_norm(y, g, b):
    mu = jnp.mean(y, axis=-1, keepdims=True)
    d = y - mu
    var = jnp.mean(d * d, axis=-1, keepdims=True)
    return d * lax.rsqrt(var + LN_EPS) * g + b


def _proj_kernel(x_ref, w_ref, o_ref):
    o_ref[...] = _dot(x_ref[...].astype(BF16), w_ref[...]).astype(o_ref.dtype)


def _proj(x, w):
    t, k = x.shape
    n = w.shape[1]
    return pl.pallas_call(
        _proj_kernel,
        out_shape=jax.ShapeDtypeStruct((t, n), BF16),
        grid=(t // ROW_BLOCK,),
        in_specs=[pl.BlockSpec((ROW_BLOCK, k), lambda i: (i, 0)),
                  _resident((k, n), lambda i: (0, 0))],
        out_specs=pl.BlockSpec((ROW_BLOCK, n), lambda i: (i, 0)),
        compiler_params=_params("parallel"),
        name="proj",
    )(x, w)


def _proj_kvf_kernel(x_ref, w_ref, wf_ref, o_ref, f_ref, kn_ref, *, dh):
    x = x_ref[...]
    kv = _dot(x.astype(BF16), w_ref[...])
    o_ref[...] = kv.astype(o_ref.dtype)
    xh, xl = _split2(x)
    a = _dot_nt(wf_ref[...], xh)
    b = _dot_nt(wf_ref[...], xl)
    f_ref[...] = a[0:8, :] + a[64:72, :] + b[0:8, :]
    lane = lax.broadcasted_iota(jnp.int32, kn_ref.shape, 1)
    best = jnp.zeros(kn_ref.shape, F32)
    for h in range(N_HEADS):
        kh = kv[:, h * dh:(h + 1) * dh]
        best = jnp.where(lane == h, jnp.max(jnp.sum(kh * kh, axis=-1, keepdims=True)), best)
    kn_ref[...] = best


def _proj_kvf(x, w, wf2t):
    t, k = x.shape
    n = w.shape[1]
    nblk = t // ROW_BLOCK
    return pl.pallas_call(
        functools.partial(_proj_kvf_kernel, dh=n // (2 * N_HEADS)),
        out_shape=(jax.ShapeDtypeStruct((t, n), BF16),
                   jax.ShapeDtypeStruct((8, t), F32),
                   jax.ShapeDtypeStruct((nblk, 8, V7X_LANES), F32)),
        grid=(nblk,),
        in_specs=[pl.BlockSpec((ROW_BLOCK, k), lambda i: (i, 0)),
                  _resident((k, n), lambda i: (0, 0)),
                  _resident((V7X_LANES, k), lambda i: (0, 0))],
        out_specs=(pl.BlockSpec((ROW_BLOCK, n), lambda i: (i, 0)),
                   pl.BlockSpec((8, ROW_BLOCK), lambda i: (0, i)),
                   pl.BlockSpec((None, 8, V7X_LANES), lambda i: (i, 0, 0))),
        compiler_params=_params("parallel"),
        name="proj_kvf",
    )(x, w, wf2t)


def _cumsum_kernel(v_ref, bias_ref, c_ref, *, rows_per_seq):
    z = v_ref[...] + bias_ref[...]
    a = (jnp.minimum(z, 0.0) - jnp.log(1.0 + jnp.exp(-jnp.abs(z)))) * LOG2E
    r = a.shape[0]
    j = lax.broadcasted_iota(jnp.int32, (V7X_LANES, V7X_LANES), 0)
    s = lax.broadcasted_iota(jnp.int32, (V7X_LANES, V7X_LANES), 1)
    upper = jnp.where(j <= s, 1.0, 0.0).astype(BF16)
    a1, a2, a3 = _split3(a)
    within = _dot(a1, upper) + _dot(a2, upper) + _dot(a3, upper)
    ri = lax.broadcasted_iota(jnp.int32, (r, r), 0)
    rj = lax.broadcasted_iota(jnp.int32, (r, r), 1)
    same = (ri // rows_per_seq) == (rj // rows_per_seq)
    before = jnp.where(same & (rj < ri), 1.0, 0.0).astype(BF16)
    w1, w2, w3 = _split3(within)
    offs = _dot(before, w1) + _dot(before, w2) + _dot(before, w3)
    c_ref[...] = within + offs[:, V7X_LANES - 1:V7X_LANES]


def _log_forget_cumsum(f_logit_t, bias_f, bsz, seq):
    rows_per_seq = seq // V7X_LANES
    fl = f_logit_t[:N_HEADS].reshape(N_HEADS * bsz * rows_per_seq, V7X_LANES)
    bias = jnp.broadcast_to(bias_f.astype(F32)[:, None, None, None],
                            (N_HEADS, bsz, rows_per_seq, 1)).reshape(-1, 1)
    c = pl.pallas_call(
        functools.partial(_cumsum_kernel, rows_per_seq=rows_per_seq),
        out_shape=jax.ShapeDtypeStruct(fl.shape, F32),
        compiler_params=_params(),
        name="forget_cumsum",
    )(fl, bias)
    return c.reshape(N_HEADS, bsz, seq).transpose(1, 0, 2)


def _softplus2(z):
    return jnp.maximum(z, 0.0) + jnp.log(1.0 + jnp.exp2(-jnp.abs(z))) * LOG2E


def _sb_attn_kernel(q_ref, k_ref, v_ref, o_ref, acc_ref, carry_ref, *, blk):
    qi = pl.program_id(2)
    groups = q_ref.shape[0] // blk
    first_diag = qi * groups
    row = lax.broadcasted_iota(jnp.int32, (blk, blk), 0)
    col = lax.broadcasted_iota(jnp.int32, (blk, blk), 1)
    causal = col < row
    suffix = jnp.where(row >= col, 1.0, 0.0).astype(BF16)

    def group_tile(g, first_blk, masks, first=False):
        n = len(masks)
        rows = pl.ds(g * blk, blk)
        start = pl.multiple_of(first_blk * blk, blk)
        k = k_ref[pl.ds(start, n * blk), :]
        v = v_ref[pl.ds(start, n * blk), :]
        z = _dot_nt(q_ref[rows, :], k)
        later = None if first else carry_ref[rows, :]
        ps = [None] * n
        for i in reversed(range(n)):
            zi = z[:, i * blk:(i + 1) * blk]
            if masks[i] is not None:
                zi = jnp.where(masks[i], zi, -jnp.inf)
            sp = _softplus2(zi)
            r = _dot(sp.astype(BF16), suffix)
            if later is not None:
                r = r + later
            ps[i] = jnp.exp2(zi - r).astype(BF16)
            later = r[:, 0:1]
        pv = _dot(jnp.concatenate(ps, axis=1) if n > 1 else ps[0], v)
        if first:
            acc_ref[rows, :] = pv
        else:
            acc_ref[rows, :] += pv
        carry_ref[rows, :] = later

    @pl.when(qi == 0)
    def _():
        group_tile(0, 0, [causal], first=True)
        for g in range(1, groups):
            group_tile(g, g - 1, [None, causal], first=True)

    @pl.when(qi > 0)
    def _():
        for g in range(groups):
            group_tile(g, first_diag + g - 1, [None, causal], first=True)

    def live():
        return jnp.min(carry_ref[...]) < UNDERFLOW_EXP2

    @pl.when(live())
    def _():
        @pl.when(qi > 0)
        def _():
            for g in range(1, groups):
                group_tile(g, first_diag - 1, [None] * g)

        if groups > 2:
            @pl.when(qi == 0)
            def _():
                for g in range(2, groups):
                    group_tile(g, 0, [None] * (g - 1))

        def body(state):
            n, _ = state
            for g in range(groups):
                group_tile(g, n - 2, [None, None])
            return n - 2, live()

        n, alive = lax.while_loop(lambda st: (st[0] >= 2) & st[1], body,
                                  (jnp.maximum(first_diag - 1, 0), live()))

        @pl.when((n == 1) & alive)
        def _():
            for g in range(groups):
                group_tile(g, 0, [None])

    o_ref[...] = acc_ref[...].astype(o_ref.dtype)


def _sb_attention(qkv, d_model):
    b, s, _ = qkv.shape
    dh = d_model // N_HEADS
    tq = ATTN_Q_TILE
    return pl.pallas_call(
        functools.partial(_sb_attn_kernel, blk=ATTN_BLOCK),
        out_shape=jax.ShapeDtypeStruct((b, s, d_model), BF16),
        grid=(b, N_HEADS, s // tq),
        in_specs=[pl.BlockSpec((None, tq, dh), lambda bi, h, qi: (bi, qi, h)),
                  pl.BlockSpec((None, s, dh), lambda bi, h, qi: (bi, 0, N_HEADS + h)),
                  pl.BlockSpec((None, s, dh), lambda bi, h, qi: (bi, 0, 2 * N_HEADS + h))],
        out_specs=pl.BlockSpec((None, tq, dh), lambda bi, h, qi: (bi, qi, h)),
        scratch_shapes=[pltpu.VMEM((tq, dh), F32), pltpu.VMEM((tq, 1), F32)],
        compiler_params=_params("parallel", "parallel", "arbitrary"),
        name="sb_attention",
    )(qkv, qkv, qkv)


def _row_norm_max(x):
    xf = x.astype(F32)
    return jnp.max(jnp.sqrt(jnp.sum(xf * xf, axis=-1, keepdims=True)))


def _fox_attn_kernel(cs_ref, ce_ref, kn_ref, q_ref, k_ref, v_ref, c_ref, o_ref, acc_ref, m_ref, l_ref, *, blk):
    bi, h, qi = pl.program_id(0), pl.program_id(1), pl.program_id(2)
    q = q_ref[...]
    groups = q.shape[0] // blk
    first_diag = qi * groups
    row = lax.broadcasted_iota(jnp.int32, (blk, blk), 0)
    col = lax.broadcasted_iota(jnp.int32, (blk, blk), 1)
    causal = col <= row

    qk_bound = _row_norm_max(q) * kn_ref[bi, h]
    c_first = [cs_ref[bi, h, first_diag + g] for g in range(groups)]

    def group_tile(g, first_blk, masks, new_max, first=False):
        n = len(masks)
        rows = pl.ds(g * blk, blk)
        start = pl.multiple_of(first_blk * blk, blk)
        k = k_ref[pl.ds(start, n * blk), :]
        v = v_ref[pl.ds(start, n * blk), :]
        z = _dot_nt(q_ref[rows, :], k)
        zs = []
        for i in range(n):
            zi = z[:, i * blk:(i + 1) * blk] + (c_first[g] - c_ref[pl.ds(first_blk + i, 1), :])
            if masks[i] is not None:
                zi = jnp.where(masks[i], zi, -jnp.inf)
            zs.append(zi)
        z = jnp.concatenate(zs, axis=1) if n > 1 else zs[0]
        if first:
            m_new = jnp.max(z, axis=-1, keepdims=True)
        elif new_max:
            m_old = m_ref[rows, :]
            m_new = jnp.maximum(m_old, jnp.max(z, axis=-1, keepdims=True))
        else:
            m_new = m_ref[rows, :]
        if first or new_max:
            m_ref[rows, :] = m_new
        p = jnp.exp2(z - m_new)
        lanes = sum(p[:, j * V7X_LANES:(j + 1) * V7X_LANES] for j in range(p.shape[1] // V7X_LANES))
        pv = _dot(p.astype(BF16), v)
        if first:
            l_ref[rows, :] = lanes
            acc_ref[rows, :] = pv
        elif new_max:
            alpha = jnp.exp2(m_old - m_new)
            l_ref[rows, :] = alpha * l_ref[rows, :] + lanes
            acc_ref[rows, :] = alpha * acc_ref[rows, :] + pv
        else:
            l_ref[rows, :] += lanes
            acc_ref[rows, :] += pv

    @pl.when(qi == 0)
    def _():
        group_tile(0, 0, [causal], True, first=True)
        for g in range(1, groups):
            group_tile(g, g - 1, [None, causal], True, first=True)

    @pl.when(qi > 0)
    def _():
        for g in range(groups):
            group_tile(g, first_diag + g - 1, [None, causal], True, first=True)

    def per_row(scalars):
        return jnp.concatenate([jnp.full((blk, 1), s, F32) for s in scalars], axis=0)

    c_first_rows = per_row(c_first)

    def row_slack():
        return jnp.max(c_first_rows - m_ref[...])

    def live(n, slack):
        return qk_bound + slack - ce_ref[bi, h, jnp.maximum(n - 1, 0)] > -UNDERFLOW_EXP2

    c_end_rows = per_row([ce_ref[bi, h, jnp.maximum(first_diag + g - 2, 0)] for g in range(groups)])
    fixed_m = qk_bound + jnp.max(c_first_rows - c_end_rows - m_ref[...]) < OVERFLOW_SAFE_EXP2
    slack0 = row_slack()

    def sweep(new_max):
        def slack_now():
            return row_slack() if new_max else slack0

        @pl.when(live(first_diag + groups - 2, slack0) & (qi > 0))
        def _():
            for g in range(1, groups):
                group_tile(g, first_diag - 1, [None] * g, new_max)

        if groups > 2:
            @pl.when(live(groups - 2, slack0) & (qi == 0))
            def _():
                for g in range(2, groups):
                    group_tile(g, 0, [None] * (g - 1), new_max)

        def body(state):
            n, _ = state
            for g in range(groups):
                group_tile(g, n - 2, [None, None], new_max)
            return n - 2, live(n - 2, slack_now())

        n0 = jnp.maximum(first_diag - 1, 0)
        n, alive = lax.while_loop(lambda st: (st[0] >= 2) & st[1], body, (n0, live(n0, slack_now())))

        @pl.when((n == 1) & alive)
        def _():
            for g in range(groups):
                group_tile(g, 0, [None], new_max)

    @pl.when(fixed_m)
    def _():
        sweep(False)

    @pl.when(jnp.logical_not(fixed_m))
    def _():
        sweep(True)

    l = jnp.sum(l_ref[...], axis=-1, keepdims=True)
    o_ref[...] = (acc_ref[...] / l).astype(o_ref.dtype)


def _fox_attention(q, kv, c, key_norm, d_model):
    b, s, _ = q.shape
    dh = d_model // N_HEADS
    blk = ATTN_BLOCK
    tq = ATTN_Q_TILE
    return pl.pallas_call(
        functools.partial(_fox_attn_kernel, blk=blk),
        out_shape=jax.ShapeDtypeStruct((b, s, d_model), BF16),
        grid=(b, N_HEADS, s // tq),
        in_specs=[pl.BlockSpec(memory_space=pltpu.SMEM),
                  pl.BlockSpec(memory_space=pltpu.SMEM),
                  pl.BlockSpec(memory_space=pltpu.SMEM),
                  pl.BlockSpec((None, tq, dh), lambda bi, h, qi: (bi, qi, h)),
                  pl.BlockSpec((None, s, dh), lambda bi, h, qi: (bi, 0, h)),
                  pl.BlockSpec((None, s, dh), lambda bi, h, qi: (bi, 0, N_HEADS + h)),
                  _resident((None, None, s // blk, blk), lambda bi, h, qi: (bi, h, 0, 0))],
        out_specs=pl.BlockSpec((None, tq, dh), lambda bi, h, qi: (bi, qi, h)),
        scratch_shapes=[pltpu.VMEM((tq, dh), F32), pltpu.VMEM((tq, 1), F32), pltpu.VMEM((tq, V7X_LANES), F32)],
        compiler_params=_params("parallel", "parallel", "arbitrary"),
        name="fox_attention",
    )(c[..., 0], c[..., -1], key_norm, q, kv, kv, c)


def _silu(g):
    return g / (1.0 + jnp.exp(-g))


def _ln_ple(x, ff, g, b, p, w_proj, w_gate, alpha):
    y = _layer_norm(alpha * x + ff, g, b)
    gate = 1.0 / (1.0 + jnp.exp(-_dot(y.astype(BF16), w_gate)))
    return y + _dot(p.astype(BF16), w_proj) * gate


def _mix_ln(x_ref, a_ref, wo_ref, g_ref, b_ref, alpha):
    return _layer_norm(alpha * x_ref[...] + _dot(a_ref[...], wo_ref[...]), g_ref[...], b_ref[...])


def _dense_ffn_kernel(x_ref, a_ref, wo_ref, g1_ref, b1_ref, wgu_ref, wd_ref, g_ref, b_ref, p_ref, wp_ref, wg_ref,
                      o_ref, *, alpha, d_ff):
    x = _mix_ln(x_ref, a_ref, wo_ref, g1_ref, b1_ref, alpha)
    h = _dot(x.astype(BF16), wgu_ref[...])
    act = (_silu(h[:, :d_ff]) * h[:, d_ff:]).astype(BF16)
    ff = _dot(act, wd_ref[...])
    o_ref[...] = _ln_ple(x, ff, g_ref[...], b_ref[...], p_ref[...], wp_ref[...], wg_ref[...], alpha)


def _dense_ffn(x, a, w_o, g1, b1, w_gu, w_down, g, b, p, layer, w_proj, w_gate, alpha):
    t, d = x.shape
    d_ff = w_down.shape[0]
    pd = p.shape[2]
    tm = ROW_BLOCK
    const = lambda i: (0, 0)
    return pl.pallas_call(
        functools.partial(_dense_ffn_kernel, alpha=alpha, d_ff=d_ff),
        out_shape=jax.ShapeDtypeStruct((t, d), F32),
        grid=(t // tm,),
        in_specs=[pl.BlockSpec((tm, d), lambda i: (i, 0)),
                  pl.BlockSpec((tm, d), lambda i: (i, 0)),
                  _resident((d, d), const),
                  _resident((1, d), const),
                  _resident((1, d), const),
                  _resident((d, 2 * d_ff), const),
                  _resident((d_ff, d), const),
                  _resident((1, d), const),
                  _resident((1, d), const),
                  pl.BlockSpec((None, tm, pd), lambda i: (layer, i, 0)),
                  _resident((pd, d), const),
                  _resident((d, d), const)],
        out_specs=pl.BlockSpec((tm, d), lambda i: (i, 0)),
        compiler_params=_params("parallel"),
        name="dense_ffn",
    )(x, a, w_o, g1, b1, w_gu, w_down, g, b, p, w_proj, w_gate)


def _top2_gates(logits):
    lane = lax.broadcasted_iota(jnp.int32, logits.shape, 1)
    v1 = jnp.max(logits, axis=-1, keepdims=True)
    i1 = jnp.min(jnp.where(logits == v1, lane, V7X_LANES), axis=-1, keepdims=True)
    rest = jnp.where(lane == i1, -jnp.inf, logits)
    v2 = jnp.max(rest, axis=-1, keepdims=True)
    i2 = jnp.min(jnp.where(rest == v2, lane, V7X_LANES), axis=-1, keepdims=True)
    e2 = jnp.exp(v2 - v1)
    g1 = 1.0 / (1.0 + e2)
    g2 = e2 / (1.0 + e2)
    return i1, g1, i2, g2


def _moe_ffn_kernel(x_ref, a_ref, wo_ref, g1_ref, b1_ref, wr_ref, wg_ref, wu_ref, wd_ref, g_ref, b_ref, p_ref,
                    wp_ref, wgate_ref, o_ref, acc_ref, route_ref, x1_ref, xb_ref, *, alpha, f_exp):
    pr = pl.program_id(1)
    tm = x_ref.shape[0]

    @pl.when(pr == 0)
    def _():
        x = _mix_ln(x_ref, a_ref, wo_ref, g1_ref, b1_ref, alpha)
        x1_ref[...] = x
        xb_ref[...] = x.astype(BF16)
        lane = lax.broadcasted_iota(jnp.int32, (tm, V7X_LANES), 1)
        logits = jnp.where(lane < N_EXPERTS, _hilo_dot(x, wr_ref[...]), -jnp.inf)
        i1, g1, i2, g2 = _top2_gates(logits)
        route_ref[...] = jnp.where(lane == i1, g1, 0.0) + jnp.where(lane == i2, g2, 0.0)
        acc_ref[...] = jnp.zeros_like(acc_ref)

    xb = xb_ref[...]
    act = _silu(_dot(xb, wg_ref[...])) * _dot(xb, wu_ref[...])
    lane = lax.broadcasted_iota(jnp.int32, (tm, V7X_LANES), 1)
    gates = route_ref[...]
    gate_a = jnp.sum(jnp.where(lane == 2 * pr, gates, 0.0), axis=-1, keepdims=True)
    gate_b = jnp.sum(jnp.where(lane == 2 * pr + 1, gates, 0.0), axis=-1, keepdims=True)
    lo = (f_exp // V7X_LANES) * V7X_LANES
    hi = lo + V7X_LANES
    mixed = jnp.where(lane < f_exp - lo, gate_a, gate_b)
    act = jnp.concatenate([act[:, :lo] * gate_a, act[:, lo:hi] * mixed, act[:, hi:] * gate_b], axis=1)
    acc_ref[...] += _dot(act.astype(BF16), wd_ref[...])

    @pl.when(pr == pl.num_programs(1) - 1)
    def _():
        o_ref[...] = _ln_ple(x1_ref[...], acc_ref[...], g_ref[...], b_ref[...], p_ref[...], wp_ref[...],
                             wgate_ref[...], alpha)


def _moe_ffn(x, a, w_o, g1, b1, w_router2, w_g, w_u, w_d, g, b, p, layer, w_proj, w_gate, alpha):
    t, d = x.shape
    f_exp = w_d.shape[0] // N_EXPERTS
    pd = p.shape[2]
    tm = ROW_BLOCK
    const = lambda i, pr: (0, 0)
    return pl.pallas_call(
        functools.partial(_moe_ffn_kernel, alpha=alpha, f_exp=f_exp),
        out_shape=jax.ShapeDtypeStruct((t, d), F32),
        grid=(t // tm, N_EXPERTS // 2),
        in_specs=[pl.BlockSpec((tm, d), lambda i, pr: (i, 0)),
                  pl.BlockSpec((tm, d), lambda i, pr: (i, 0)),
                  _resident((d, d), const),
                  _resident((1, d), const),
                  _resident((1, d), const),
                  _resident((d, V7X_LANES), const),
                  pl.BlockSpec((d, 2 * f_exp), lambda i, pr: (0, pr)),
                  pl.BlockSpec((d, 2 * f_exp), lambda i, pr: (0, pr)),
                  pl.BlockSpec((2 * f_exp, d), lambda i, pr: (pr, 0)),
                  _resident((1, d), const),
                  _resident((1, d), const),
                  pl.BlockSpec((None, tm, pd), lambda i, pr: (layer, i, 0)),
                  _resident((pd, d), const),
                  _resident((d, d), const)],
        out_specs=pl.BlockSpec((tm, d), lambda i, pr: (i, 0)),
        scratch_shapes=[pltpu.VMEM((tm, d), F32), pltpu.VMEM((tm, V7X_LANES), F32),
                        pltpu.VMEM((tm, d), F32), pltpu.VMEM((tm, d), BF16)],
        compiler_params=_params("parallel", "arbitrary"),
        name="moe_ffn",
    )(x, a, w_o, g1, b1, w_router2, w_g, w_u, w_d, g, b, p, w_proj, w_gate)


def kernel(x, p, a_w_qkv, a_w_o, kvf_w, kvf_b_f, b_w_q, b_w_o, ffn_w_gu, ffn_w_down, moe_w_router, moe_w_gu,
           moe_w_down, ple_w_proj, ple_w_gate, ln_g, ln_b):
    bsz, seq, d = x.shape
    depth = p.shape[0]
    n_a = a_w_qkv.shape[0]
    t = bsz * seq
    dh = d // N_HEADS
    alpha = (2.0 * depth) ** 0.25
    scale = LOG2E / math.sqrt(dh)
    f_exp = moe_w_down.shape[2]

    xf = x.reshape(t, d)
    pf = p.reshape(depth, t, p.shape[-1])

    c_sh = kv_sh = kn_sh = None
    for i in range(depth):
        g1, b1 = ln_g[i, 0][None, :], ln_b[i, 0][None, :]
        g2, b2 = ln_g[i, 1][None, :], ln_b[i, 1][None, :]
        if i < n_a:
            w_qkv = jnp.concatenate([a_w_qkv[i][:, :d] * scale, a_w_qkv[i][:, d:]], axis=1).astype(BF16)
            qkv = _proj(xf, w_qkv).reshape(bsz, seq, 3 * d)
            mix = _sb_attention(qkv, d).reshape(t, d)
            w_o = a_w_o[i].astype(BF16)
        else:
            if i == n_a:
                kv, f_logit, kn2 = _proj_kvf(xf, kvf_w[:, :2 * d].astype(BF16), _hilo_weight(kvf_w[:, 2 * d:]).T)
                kv_sh = kv.reshape(bsz, seq, 2 * d)
                kn_sh = 1.02 * jnp.sqrt(jnp.max(kn2[:, 0, :N_HEADS].reshape(bsz, -1, N_HEADS), axis=1))
                c_sh = _log_forget_cumsum(f_logit, kvf_b_f, bsz, seq)
                c_sh = c_sh.reshape(bsz, N_HEADS, seq // ATTN_BLOCK, ATTN_BLOCK)
            j = i - n_a
            q = _proj(xf, (b_w_q[j] * scale).astype(BF16)).reshape(bsz, seq, d)
            mix = _fox_attention(q, kv_sh, c_sh, kn_sh, d).reshape(t, d)
            w_o = b_w_o[j].astype(BF16)

        w_proj = ple_w_proj[i].astype(BF16)
        w_gate = ple_w_gate[i].astype(BF16)
        if i % 2 == 0:
            xf = _dense_ffn(xf, mix, w_o, g1, b1, ffn_w_gu[i // 2].astype(BF16), ffn_w_down[i // 2].astype(BF16),
                            g2, b2, pf, i, w_proj, w_gate, alpha)
        else:
            m = i // 2
            w_gu = moe_w_gu[m].astype(BF16)
            w_g = w_gu[:, :, :f_exp].transpose(1, 0, 2).reshape(d, N_EXPERTS * f_exp)
            w_u = w_gu[:, :, f_exp:].transpose(1, 0, 2).reshape(d, N_EXPERTS * f_exp)
            w_d = moe_w_down[m].reshape(N_EXPERTS * f_exp, d).astype(BF16)
            xf = _moe_ffn(xf, mix, w_o, g1, b1, _hilo_weight(moe_w_router[m]), w_g, w_u, w_d, g2, b2, pf, i,
                          w_proj, w_gate, alpha)
    return xf.reshape(bsz, seq, d)
```
